```python
import math
import numpy as np
import jax
import jax.numpy as jnp
from jax import lax

D_MODEL = 4096
BATCH = 1
SEQ = 8192
DEPTH = 2

GRID_W = 64
CTX_LEN = 256
NORM_EPS = 1e-6
A_HEADS = 8
A_DH = 64
A_WIDTH = A_HEADS * 2 * A_DH
ROPE_PAIRS = A_DH // 4
ROPE_BASE = 10000.0
Q_BLOCK = 128
B_WIDTH = 1024
B_BLOCKS = 8
B_BW = B_WIDTH // B_BLOCKS
B_CONV = 4
CONV_LEFT = 2
LRU_C = 8.0
C_HEADS = 16
C_DH = 64
C_WIDTH = C_HEADS * C_DH
NA_ROWS = 8
NA_COLS = 16
N_BRANCH = 3
BRANCH_W = 1024
IN_SPLITS = (A_WIDTH, A_WIDTH, A_WIDTH, B_WIDTH, B_WIDTH, C_WIDTH, C_WIDTH, C_WIDTH, N_BRANCH * D_MODEL)
W_IN = 3 * A_WIDTH + 2 * B_WIDTH + 3 * C_WIDTH + N_BRANCH * D_MODEL
N_EXPERTS = 32
TOP_K = 4
F_EXPERT = 512
SWIGLU_ALPHA = 1.702
SWIGLU_LIMIT = 7.0
MOE_BLOCK = 128

kernel_name = "hybrid_gated_diffusion_block"


def rmsnorm(x, g):
    xf = x.astype(jnp.float32)
    y = xf * lax.rsqrt(jnp.mean(xf * xf, axis=-1, keepdims=True) + NORM_EPS)
    return (y * g.astype(jnp.float32)).astype(x.dtype)


def modulate(h, shift, scale):
    return h * (1.0 + scale) + shift


def split_in(z):
    return jnp.split(z, np.cumsum(IN_SPLITS)[:-1].tolist(), axis=-1)


def axial_angles(n):
    t = jnp.arange(n)
    row = (t // GRID_W).astype(jnp.float32)
    col = (t % GRID_W).astype(jnp.float32)
    inv = ROPE_BASE ** (-jnp.arange(ROPE_PAIRS, dtype=jnp.float32) / ROPE_PAIRS)
    return row[:, None] * inv, col[:, None] * inv


def rotate(x, ang):
    cos = jnp.cos(ang)[:, None, None, :].astype(x.dtype)
    sin = jnp.sin(ang)[:, None, None, :].astype(x.dtype)
    x1, x2 = x[..., :ROPE_PAIRS], x[..., ROPE_PAIRS:]
    return jnp.concatenate([x1 * cos - x2 * sin, x2 * cos + x1 * sin], axis=-1)


def axial_rope(x, ang_r, ang_c):
    half = A_DH // 2
    return jnp.concatenate([rotate(x[..., :half], ang_r), rotate(x[..., half:], ang_c)], axis=-1)


def diff_attend(q, k, v, lam):
    s = jnp.einsum('bqhcd,bkhcd->bhcqk', q, k).astype(jnp.float32) * (A_DH ** -0.5)
    p = jax.nn.softmax(s, axis=-1)
    w = (p[:, :, 0] - lam * p[:, :, 1]).astype(v.dtype)
    return jnp.einsum('bhqk,bkhe->bqhe', w, v)


def diff_attention(q, k, v, qc, kc, vc, lam_qk, subln_g, lambda_init):
    B, n = q.shape[0], q.shape[1]
    lq = lam_qk.astype(jnp.float32)
    lam = jnp.exp(jnp.sum(lq[0] * lq[1])) - jnp.exp(jnp.sum(lq[2] * lq[3])) + lambda_init
    k_all = jnp.concatenate([kc, k], axis=1)
    v_all = jnp.concatenate([vc, v], axis=1)
    qb = jnp.moveaxis(q.reshape(B, n // Q_BLOCK, Q_BLOCK, A_HEADS, 2, A_DH), 1, 0)
    o = lax.map(lambda qq: diff_attend(qq, k_all, v_all, lam), qb)
    o = jnp.moveaxis(o, 0, 1).reshape(B, n, A_HEADS, 2 * A_DH)
    oc = diff_attend(qc, kc, vc, lam)

    def post(t):
        return (rmsnorm(t, subln_g) * (1.0 - lambda_init)).reshape(t.shape[0], t.shape[1], A_WIDTH)
    return post(o), post(oc)


def short_conv(x, w, b):
    n = x.shape[1]
    xp = jnp.pad(x, ((0, 0), (CONV_LEFT, B_CONV - 1 - CONV_LEFT), (0, 0)))
    y = b + xp[:, 0:n] * w[0]
    for j in range(1, B_CONV):
        y = y + xp[:, j:j + n] * w[j]
    return y


def rglru_coeffs(u, gw, gb, lam):
    B, L, _ = u.shape
    g = jnp.einsum('blkc,gkcd->gblkd', u.reshape(B, L, B_BLOCKS, B_BW), gw.astype(jnp.float32))
    g = g + gb.astype(jnp.float32)[:, None, None]
    r = jax.nn.sigmoid(g[0]).reshape(B, L, B_WIDTH)
    i = jax.nn.sigmoid(g[1]).reshape(B, L, B_WIDTH)
    log_a = -LRU_C * r * jax.nn.softplus(-lam.astype(jnp.float32))
    a = jnp.exp(log_a)
    b = jnp.sqrt(-jnp.expm1(2.0 * log_a)) * (i * u)
    return a, b


def linear_scan(a, b, h0, reverse):
    if h0 is not None:
        edge = b.shape[1] - 1 if reverse else 0
        b = b.at[:, edge].add(a[:, edge] * h0)

    def combine(left, right):
        a_l, b_l = left
        a_r, b_r = right
        return a_l * a_r, a_r * b_l + b_r
    _, h = lax.associative_scan(combine, (a, b), axis=1, reverse=reverse)
    return h


def rglru_mixer(bx, by, bxc, byc, conv_w, conv_b, gate_w, gate_b, lru_lambda):
    u = short_conv(bx, conv_w, conv_b).astype(jnp.float32)
    uc = short_conv(bxc, conv_w, conv_b).astype(jnp.float32)
    hs, hcs = [], []
    for d, rev in enumerate((False, True)):
        a_c, b_c = rglru_coeffs(uc, gate_w[d], gate_b[d], lru_lambda[d])
        h_c = linear_scan(a_c, b_c, None, rev)
        h_end = h_c[:, 0] if rev else h_c[:, -1]
        a_l, b_l = rglru_coeffs(u, gate_w[d], gate_b[d], lru_lambda[d])
        hs.append(linear_scan(a_l, b_l, h_end, rev))
        hcs.append(h_c)
    y = (hs[0] + hs[1]).astype(bx.dtype) * jax.nn.gelu(by)
    yc = (hcs[0] + hcs[1]).astype(bxc.dtype) * jax.nn.gelu(byc)
    return y, yc


def neighborhood_attention(q, k, v, qc, kc, vc, rpb):
    B, n, H, dh = q.shape
    m = qc.shape[1]
    rows = n // GRID_W
    kr = min(NA_ROWS, rows)
    nk = kr * NA_COLS
    scale = dh ** -0.5
    qg = q.reshape(B, rows, GRID_W, H, dh)
    kg = k.reshape(B, rows, GRID_W, H, dh)
    vg = v.reshape(B, rows, GRID_W, H, dh)
    col = jnp.arange(GRID_W)
    col_idx = jnp.clip(col - NA_COLS // 2, 0, GRID_W - NA_COLS)[:, None] + jnp.arange(NA_COLS)
    col_rel = col_idx - col[:, None] + (NA_COLS - 1)
    rpb_f = rpb.astype(jnp.float32)

    def row_block(r):
        r0 = jnp.clip(r - kr // 2, 0, rows - kr)
        kb = lax.dynamic_slice_in_dim(kg, r0, kr, axis=1)[:, :, col_idx]
        vb = lax.dynamic_slice_in_dim(vg, r0, kr, axis=1)[:, :, col_idx]
        qr = lax.dynamic_index_in_dim(qg, r, axis=1, keepdims=False)
        row_rel = r0 + jnp.arange(kr) - r + (NA_ROWS - 1)
        bias = rpb_f[:, row_rel[None, :, None], col_rel[:, None, :]]
        s_nb = jnp.einsum('bqhd,brqchd->bhqrc', qr, kb).astype(jnp.float32) * scale + bias
        s_cx = jnp.einsum('bqhd,bkhd->bhqk', qr, kc).astype(jnp.float32) * scale
        s = jnp.concatenate([s_nb.reshape(B, H, GRID_W, nk), s_cx], axis=-1)
        p = jax.nn.softmax(s, axis=-1).astype(v.dtype)
        p_nb = p[..., :nk].reshape(B, H, GRID_W, kr, NA_COLS)
        return (jnp.einsum('bhqrc,brqchd->bqhd', p_nb, vb)
                + jnp.einsum('bhqk,bkhd->bqhd', p[..., nk:], vc))

    o = lax.map(row_block, jnp.arange(rows))
    o = jnp.moveaxis(o, 0, 1).reshape(B, n, H * dh)
    sc = jax.nn.softmax(jnp.einsum('bqhd,bkhd->bhqk', qc, kc).astype(jnp.float32) * scale, axis=-1)
    oc = jnp.einsum('bhqk,bkhd->bqhd', sc.astype(vc.dtype), vc).reshape(B, m, H * dh)
    return o, oc


def hybrid_mixer(h, hc, w_in, lam_qk, subln_g, conv_w, conv_b, lru_gate_w, lru_gate_b, lru_lambda,
                 na_rpb, w_branch, w_out, lambda_init):
    B, n, _ = h.shape
    m = hc.shape[1]
    aq, ak, av, bx, by, cq, ck, cv, g = split_in(h @ w_in)
    aqc, akc, avc, bxc, byc, cqc, ckc, cvc, gc = split_in(hc @ w_in)

    ang_r, ang_c = axial_angles(n)
    qa = axial_rope(aq.reshape(B, n, A_HEADS, 2, A_DH), ang_r, ang_c)
    ka = axial_rope(ak.reshape(B, n, A_HEADS, 2, A_DH), ang_r, ang_c)
    o_a, o_ac = diff_attention(qa, ka, av.reshape(B, n, A_HEADS, 2 * A_DH),
                               aqc.reshape(B, m, A_HEADS, 2, A_DH), akc.reshape(B, m, A_HEADS, 2, A_DH),
                               avc.reshape(B, m, A_HEADS, 2 * A_DH), lam_qk, subln_g, lambda_init)

    o_b, o_bc = rglru_mixer(bx, by, bxc, byc, conv_w, conv_b, lru_gate_w, lru_gate_b, lru_lambda)

    o_c, o_cc = neighborhood_attention(
        cq.reshape(B, n, C_HEADS, C_DH), ck.reshape(B, n, C_HEADS, C_DH), cv.reshape(B, n, C_HEADS, C_DH),
        cqc.reshape(B, m, C_HEADS, C_DH), ckc.reshape(B, m, C_HEADS, C_DH), cvc.reshape(B, m, C_HEADS, C_DH),
        na_rpb)

    def merge(outs, gates):
        y = None
        for i, o in enumerate(outs):
            term = jax.nn.sigmoid(gates[..., i * D_MODEL:(i + 1) * D_MODEL]) * (o @ w_branch[i])
            y = term if y is None else y + term
        return y @ w_out
    return merge((o_a, o_b, o_c), g), merge((o_ac, o_bc, o_cc), gc)


def clamped_swiglu(gu):
    glu = jnp.minimum(gu[..., ::2], SWIGLU_LIMIT)
    lin = jnp.clip(gu[..., 1::2], -SWIGLU_LIMIT, SWIGLU_LIMIT)
    return glu * jax.nn.sigmoid(SWIGLU_ALPHA * glu) * (lin + 1.0)


def moe_ffn(h, w_router, b_router, w_gu, b_gu, w_down, b_down):
    T = h.shape[0]
    tk = T * TOP_K
    logits = (h @ w_router).astype(jnp.float32) + b_router.astype(jnp.float32)
    top_val, top_idx = lax.top_k(logits, TOP_K)
    gate = jax.nn.softmax(top_val, axis=-1)
    flat_e = top_idx.reshape(tk)
    flat_tok = jnp.arange(tk, dtype=jnp.int32) // TOP_K
    flat_w = gate.reshape(tk)
    order = jnp.argsort(flat_e)
    sorted_e = flat_e[order]
    counts = jnp.bincount(flat_e, length=N_EXPERTS)
    padded = (counts + MOE_BLOCK - 1) // MOE_BLOCK * MOE_BLOCK
    start = jnp.cumsum(counts) - counts
    pend = jnp.cumsum(padded)
    pstart = pend - padded
    dest = pstart[sorted_e] + jnp.arange(tk) - start[sorted_e]
    n_blocks = -(-(tk + N_EXPERTS * (MOE_BLOCK - 1)) // MOE_BLOCK)
    n_rows = n_blocks * MOE_BLOCK
    row_tok = jnp.zeros((n_rows,), jnp.int32).at[dest].set(flat_tok[order])
    row_w = jnp.zeros((n_rows,), jnp.float32).at[dest].set(flat_w[order])
    block_e = jnp.minimum(jnp.searchsorted(pend, jnp.arange(n_blocks) * MOE_BLOCK, side='right'),
                          N_EXPERTS - 1)

    def body(i, acc):
        e = block_e[i]
        toks = lax.dynamic_slice_in_dim(row_tok, i * MOE_BLOCK, MOE_BLOCK)
        wts = lax.dynamic_slice_in_dim(row_w, i * MOE_BLOCK, MOE_BLOCK)
        gu = h[toks] @ w_gu[e] + b_gu[e]
        y = clamped_swiglu(gu) @ w_down[e] + b_down[e]
        return acc.at[toks].add(y * wts[:, None].astype(y.dtype))
    return lax.fori_loop(0, n_blocks, body, jnp.zeros_like(h))


def trunk_layer(x, xc, c, c_ctx, w_ada, b_ada, norm1_g, norm2_g, w_in, lam_qk, subln_g, conv_w, conv_b,
                lru_gate_w, lru_gate_b, lru_lambda, na_rpb, w_branch, w_out, w_router, b_router,
                w_gu, b_gu, w_down, b_down, lambda_init):
    B, n, D = x.shape
    m = xc.shape[1]
    mod = jax.nn.silu(c) @ w_ada + b_ada
    mod_c = jax.nn.silu(c_ctx) @ w_ada + b_ada
    sh1, sc1, g1, sh2, sc2, g2 = jnp.split(mod[:, None, :], 6, axis=-1)
    sh1c, sc1c, g1c, sh2c, sc2c, g2c = jnp.split(mod_c, 6)
    mix, mix_c = hybrid_mixer(modulate(rmsnorm(x, norm1_g), sh1, sc1),
                              modulate(rmsnorm(xc, norm1_g), sh1c, sc1c),
                              w_in, lam_qk, subln_g, conv_w, conv_b, lru_gate_w, lru_gate_b, lru_lambda,
                              na_rpb, w_branch, w_out, lambda_init)
    x = x + g1 * mix
    xc = xc + g1c * mix_c
    tokens = jnp.concatenate([modulate(rmsnorm(xc, norm2_g), sh2c, sc2c),
                              modulate(rmsnorm(x, norm2_g), sh2, sc2)], axis=1)
    f = moe_ffn(tokens.reshape(B * (m + n), D), w_router, b_router, w_gu, b_gu, w_down, b_down)
    f = f.reshape(B, m + n, D)
    return x + g2 * f[:, m:], xc + g2c * f[:, :m]


def setup_inputs(seed: int = 0) -> dict:
    key = jax.random.key(seed)
    ks = jax.random.split(key, 28)
    f32 = jnp.float32
    D, L, E, F = D_MODEL, DEPTH, N_EXPERTS, F_EXPERT

    def nrm(k, shape, scale):
        return jax.random.normal(k, shape, f32) * scale

    a0 = jax.random.uniform(ks[16], (L, 2, B_WIDTH), f32, minval=0.9, maxval=0.999)
    s0 = a0 ** (1.0 / LRU_C)
    return {
        "x": nrm(ks[0], (BATCH, SEQ, D), 1.0),
        "c": nrm(ks[1], (BATCH, D), 1.0),
        "ctx": nrm(ks[2], (BATCH, CTX_LEN, D), 1.0),
        "c_ctx": nrm(ks[3], (D,), 1.0),
        "w_ada": nrm(ks[4], (L, D, 6 * D), 0.5 * D ** -0.5),
        "b_ada": nrm(ks[5], (L, 6 * D), 0.02),
        "norm1_g": 1.0 + nrm(ks[6], (L, D), 0.05),
        "norm2_g": 1.0 + nrm(ks[7], (L, D), 0.05),
        "w_in": nrm(ks[8], (L, D, W_IN), D ** -0.5),
        "lam_qk": nrm(ks[9], (L, 4, A_DH), 0.1),
        "subln_g": 1.0 + nrm(ks[10], (L, 2 * A_DH), 0.05),
        "conv_w": nrm(ks[11], (L, B_CONV, B_WIDTH), B_CONV ** -0.5),
        "conv_b": nrm(ks[12], (L, B_WIDTH), 0.02),
        "lru_gate_w": nrm(ks[13], (L, 2, 2, B_BLOCKS, B_BW, B_BW), B_BW ** -0.5),
        "lru_gate_b": nrm(ks[14], (L, 2, 2, B_BLOCKS, B_BW), 0.02),
        "lru_lambda": jnp.log(s0) - jnp.log1p(-s0),
        "na_rpb": nrm(ks[15], (L, C_HEADS, 2 * NA_ROWS - 1, 2 * NA_COLS - 1), 0.1),
        "w_branch": nrm(ks[17], (L, N_BRANCH, BRANCH_W, D), BRANCH_W ** -0.5),
        "w_out": nrm(ks[18], (L, D, D), D ** -0.5),
        "w_router": nrm(ks[19], (L, D, E), D ** -0.5),
        "b_router": nrm(ks[20], (L, E), 0.01),
        "w_gu": nrm(ks[21], (L, E, D, 2 * F), D ** -0.5),
        "b_gu": nrm(ks[22], (L, E, 2 * F), 0.02),
        "w_down": nrm(ks[23], (L, E, F, D), F ** -0.5),
        "b_down": nrm(ks[24], (L, E, D), 0.02),
        "final_g": 1.0 + nrm(ks[25], (D,), 0.05),
    }


def reference(x, c, ctx, c_ctx, w_ada, b_ada, norm1_g, norm2_g, w_in, lam_qk, subln_g, conv_w, conv_b,
              lru_gate_w, lru_gate_b, lru_lambda, na_rpb, w_branch, w_out, w_router, b_router,
              w_gu, b_gu, w_down, b_down, final_g):
    x_lat, x_ctx = x, ctx
    for l in range(DEPTH):
        lambda_init = 0.8 - 0.6 * math.exp(-0.3 * l)
        x_lat, x_ctx = trunk_layer(x_lat, x_ctx, c, c_ctx, w_ada[l], b_ada[l], norm1_g[l], norm2_g[l],
                                   w_in[l], lam_qk[l], subln_g[l], conv_w[l], conv_b[l], lru_gate_w[l],
                                   lru_gate_b[l], lru_lambda[l], na_rpb[l], w_branch[l], w_out[l],
                                   w_router[l], b_router[l], w_gu[l], b_gu[l], w_down[l], b_down[l],
                                   lambda_init)
    return rmsnorm(x_lat, final_g)
```

```python
import functools
import math

import numpy as np
import jax
import jax.numpy as jnp
from jax import lax
from jax.experimental import pallas as pl
from jax.experimental.pallas import tpu as pltpu

F32 = jnp.float32
BF16 = jnp.bfloat16

GRID_W = 64
NORM_EPS = 1e-6
A_HEADS, A_DH = 8, 64
ROPE_PAIRS = A_DH // 4
ROPE_BASE = 10000.0
B_WIDTH, B_BW, B_CONV, CONV_LEFT, LRU_C = 1024, 128, 4, 2, 8.0
C_HEADS, C_DH, NA_ROWS, NA_COLS = 16, 64, 8, 16
N_BRANCH, BRANCH_W = 3, 1024
TOP_K = 4
SWIGLU_ALPHA, SWIGLU_LIMIT = 1.702, 7.0

LANES = 128
SUBLANES = 8
COL_AQ, COL_AK, COL_AV, COL_BX, COL_BY, COL_CQ, COL_CK, COL_CV, COL_G = 0, 8, 16, 24, 32, 40, 48, 56, 64
NEG_BIG = -1e30
VMEM_LIMIT = 56 * 1024 * 1024


def _params(sem):
    return pltpu.CompilerParams(dimension_semantics=sem, vmem_limit_bytes=VMEM_LIMIT)


def _tile(dim, target, mult):
    best = None
    for t in range(mult, min(dim, target) + 1, mult):
        if dim % t == 0:
            best = t
    assert best is not None, (dim, target, mult)
    return best


def _ada_kernel(c_ref, w_ref, b_ref, o_ref, acc_ref):
    k = pl.program_id(2)

    @pl.when(k == 0)
    def _():
        acc_ref[...] = jnp.zeros_like(acc_ref)

    cc = c_ref[...]
    s = cc * jax.nn.sigmoid(cc)
    hi = s.astype(BF16)
    lo = (s - hi.astype(F32)).astype(BF16)
    row = lax.broadcasted_iota(jnp.int32, s.shape, 0)
    lhs = jnp.where(row < 2, hi, lo)
    acc_ref[...] += jnp.dot(lhs, w_ref[...].astype(BF16), preferred_element_type=F32)

    @pl.when(k == pl.num_programs(2) - 1)
    def _():
        acc = acc_ref[...]
        o_ref[...] = acc + pltpu.roll(acc, SUBLANES - 2, 0) + b_ref[...]


def _ada(c, c_ctx, w_ada, b_ada):
    L, D, N = w_ada.shape
    cc = jnp.concatenate([c[:1], c_ctx[None], c[:1], c_ctx[None], jnp.zeros((4, D), F32)], axis=0)
    tk = _tile(D, 1024, LANES)
    tn = _tile(N, 2048, LANES)
    return pl.pallas_call(
        _ada_kernel,
        grid=(L, N // tn, D // tk),
        in_specs=[
            pl.BlockSpec((SUBLANES, tk), lambda l, j, k: (0, k)),
            pl.BlockSpec((None, tk, tn), lambda l, j, k: (l, k, j)),
            pl.BlockSpec((None, 1, tn), lambda l, j, k: (l, 0, j)),
        ],
        out_specs=pl.BlockSpec((None, SUBLANES, tn), lambda l, j, k: (l, 0, j)),
        out_shape=jax.ShapeDtypeStruct((L, SUBLANES, N), F32),
        scratch_shapes=[pltpu.VMEM((SUBLANES, tn), F32)],
        compiler_params=_params(("parallel", "parallel", "arbitrary")),
        name="ada",
    )(cc, w_ada, b_ada.reshape(L, 1, N))


def _norm_mod_kernel(x_ref, g_ref, sh_ref, sc_ref, o_ref, *, m_tiles):
    is_ctx = pl.program_id(0) < m_tiles
    x = x_ref[...]
    y = x * lax.rsqrt(jnp.mean(x * x, axis=-1, keepdims=True) + NORM_EPS) * g_ref[...]
    sh = jnp.where(is_ctx, sh_ref[1:2, :], sh_ref[0:1, :])
    sc = jnp.where(is_ctx, sc_ref[1:2, :], sc_ref[0:1, :])
    o_ref[...] = (y * (1.0 + sc) + sh).astype(o_ref.dtype)


def _norm_mod(xs, g, mod, shift_idx, scale_idx, m, tr):
    T, D = xs.shape
    return pl.pallas_call(
        functools.partial(_norm_mod_kernel, m_tiles=m // tr),
        grid=(T // tr,),
        in_specs=[
            pl.BlockSpec((tr, D), lambda i: (i, 0)),
            pl.BlockSpec((1, D), lambda i: (0, 0)),
            pl.BlockSpec((SUBLANES, D), lambda i: (0, shift_idx)),
            pl.BlockSpec((SUBLANES, D), lambda i: (0, scale_idx)),
        ],
        out_specs=pl.BlockSpec((tr, D), lambda i: (i, 0)),
        out_shape=jax.ShapeDtypeStruct((T, D), BF16),
        compiler_params=_params(("parallel",)),
        name="norm_mod",
    )(xs, g.reshape(1, D), mod, mod)


def _mm_kernel(a_ref, b_ref, o_ref):
    o_ref[...] = jnp.dot(a_ref[...], b_ref[...], preferred_element_type=F32).astype(o_ref.dtype)


def _matmul(a, b, tm, tn, out_dtype):
    M, K = a.shape
    _, N = b.shape
    return pl.pallas_call(
        _mm_kernel,
        grid=(M // tm, N // tn),
        in_specs=[pl.BlockSpec((tm, K), lambda i, j: (i, 0)), pl.BlockSpec((K, tn), lambda i, j: (0, j))],
        out_specs=pl.BlockSpec((tm, tn), lambda i, j: (i, j)),
        out_shape=jax.ShapeDtypeStruct((M, N), out_dtype),
        compiler_params=_params(("parallel", "arbitrary")),
        name="in_proj",
    )(a, b)


def _rope_tables(m, n):
    t = np.arange(n)
    row = (t // GRID_W).astype(np.float32)
    col = (t % GRID_W).astype(np.float32)
    inv = jnp.asarray(ROPE_BASE, F32) ** (-jnp.arange(ROPE_PAIRS, dtype=F32) / ROPE_PAIRS)
    ang_r = jnp.asarray(row)[:, None] * inv
    ang_c = jnp.asarray(col)[:, None] * inv
    lane = np.arange(LANES)
    d = lane % A_DH
    use_col = (d // (A_DH // 2)) == 1
    within = d % (A_DH // 2)
    pair = within % ROPE_PAIRS
    sign = np.where(within < ROPE_PAIRS, -1.0, 1.0).astype(np.float32)
    ang = jnp.where(jnp.asarray(use_col)[None, :], ang_c[:, pair], ang_r[:, pair])
    cos = jnp.concatenate([jnp.ones((m, LANES), F32), jnp.cos(ang)], axis=0)
    sin = jnp.concatenate([jnp.zeros((m, LANES), F32), jnp.sin(ang) * jnp.asarray(sign)[None, :]], axis=0)
    return cos, sin


def _rope_kernel(z_ref, cos_ref, sin_ref, o_ref):
    scale = jnp.where(pl.program_id(1) == 0, A_DH ** -0.5, 1.0).astype(F32)
    cos = cos_ref[...]
    sin = sin_ref[...]
    lane = lax.broadcasted_iota(jnp.int32, cos.shape, 1)
    lo = (lane % (2 * ROPE_PAIRS)) < ROPE_PAIRS
    for hc in range(z_ref.shape[1] // LANES):
        x = z_ref[:, hc * LANES:(hc + 1) * LANES].astype(F32)
        partner = jnp.where(lo, pltpu.roll(x, LANES - ROPE_PAIRS, 1), pltpu.roll(x, ROPE_PAIRS, 1))
        o_ref[:, hc * LANES:(hc + 1) * LANES] = ((x * cos + partner * sin) * scale).astype(o_ref.dtype)


def _rope(z, cos, sin, tr):
    T = z.shape[0]
    W = A_HEADS * 2 * A_DH
    return pl.pallas_call(
        _rope_kernel,
        grid=(T // tr, 2),
        in_specs=[
            pl.BlockSpec((tr, W), lambda i, j: (i, j)),
            pl.BlockSpec((tr, LANES), lambda i, j: (i, 0)),
            pl.BlockSpec((tr, LANES), lambda i, j: (i, 0)),
        ],
        out_specs=pl.BlockSpec((tr, W), lambda i, j: (i, j)),
        out_shape=jax.ShapeDtypeStruct((T, 2 * W), BF16),
        compiler_params=_params(("parallel", "arbitrary")),
        name="rope",
    )(z, cos, sin)


def _attn_a_kernel(lam_ref, g_ref, q_ref, k_ref, v_ref, o_ref, *, m, T, tk, lambda_init):
    qi = pl.program_id(1)
    qb = q_ref[...]
    tq = qb.shape[0]
    lane = lax.broadcasted_iota(jnp.int32, qb.shape, 1)
    zero = jnp.zeros_like(qb)
    qs = jnp.concatenate([jnp.where(lane < A_DH, qb, zero), jnp.where(lane >= A_DH, qb, zero)], axis=0)

    def step(kc, vc, carry):
        m_i, l_i, acc = carry
        s = lax.dot_general(qs, kc, (((1,), (1,)), ((), ())), preferred_element_type=F32)
        m_new = jnp.maximum(m_i, jnp.max(s, axis=-1, keepdims=True))
        alpha = jnp.exp(m_i - m_new)
        p = jnp.exp(s - m_new)
        l_new = alpha * l_i + jnp.sum(p, axis=-1, keepdims=True)
        acc = alpha * acc + jnp.dot(p.astype(BF16), vc, preferred_element_type=F32)
        return m_new, l_new, acc

    init = (jnp.full((2 * tq, 1), NEG_BIG, F32), jnp.zeros((2 * tq, 1), F32), jnp.zeros((2 * tq, LANES), F32))

    lq = lam_ref[...]
    lam = (jnp.exp(jnp.sum(lq[0:1] * lq[1:2], axis=-1, keepdims=True))
           - jnp.exp(jnp.sum(lq[2:3] * lq[3:4], axis=-1, keepdims=True)) + lambda_init)

    def finish(carry):
        _, l_i, acc = carry
        o = acc / l_i
        d = o[:tq] - lam * o[tq:]
        y = d * lax.rsqrt(jnp.mean(d * d, axis=-1, keepdims=True) + NORM_EPS) * g_ref[...]
        o_ref[...] = (y * (1.0 - lambda_init)).astype(o_ref.dtype)

    @pl.when(qi < m // tq)
    def _():
        finish(step(k_ref[0:m, :], v_ref[0:m, :], init))

    @pl.when(qi >= m // tq)
    def _():
        def body(c, carry):
            off = pl.multiple_of(c * tk, tk)
            return step(k_ref[pl.ds(off, tk), :], v_ref[pl.ds(off, tk), :], carry)

        finish(lax.fori_loop(0, T // tk, body, init))


def _attn_a(qk, z, lam_qk, subln_g, lambda_init, m, tq, tk):
    T = z.shape[0]
    return pl.pallas_call(
        functools.partial(_attn_a_kernel, m=m, T=T, tk=tk, lambda_init=lambda_init),
        grid=(A_HEADS, T // tq),
        in_specs=[
            pl.BlockSpec((4, A_DH), lambda h, i: (0, 0)),
            pl.BlockSpec((1, 2 * A_DH), lambda h, i: (0, 0)),
            pl.BlockSpec((tq, LANES), lambda h, i: (i, h)),
            pl.BlockSpec((T, LANES), lambda h, i: (0, A_HEADS + h)),
            pl.BlockSpec((T, LANES), lambda h, i: (0, COL_AV + h)),
        ],
        out_specs=pl.BlockSpec((tq, LANES), lambda h, i: (i, h)),
        out_shape=jax.ShapeDtypeStruct((T, A_HEADS * 2 * A_DH), BF16),
        compiler_params=_params(("parallel", "arbitrary")),
        name="diff_attn",
    )(lam_qk, subln_g.reshape(1, 2 * A_DH), qk, qk, z)


def _rglru_kernel(bx_ref, by_ref, cw_ref, cb_ref, gw_ref, gb_ref, lam_ref, o_ref, xpad, hf, *, m, T, tc):
    nchunk = T // tc
    mc = m // tc
    nb = tc // SUBLANES
    halo = SUBLANES
    xpad[0:halo, :] = jnp.zeros((halo, LANES), F32)
    xpad[halo + T:2 * halo + T, :] = jnp.zeros((halo, LANES), F32)
    xpad[halo:halo + T, :] = bx_ref[...].astype(F32)

    row = lax.broadcasted_iota(jnp.int32, (tc, LANES), 0)
    sub = lax.broadcasted_iota(jnp.int32, (nb, SUBLANES, LANES), 1)
    cw = cw_ref[...]
    cb = cb_ref[...]

    def coeffs(j, d):
        r0 = j * tc
        left_cut = r0 == m
        right_cut = (r0 + tc) == m
        x_m2 = xpad[pl.ds(r0 + halo - 2, tc), :]
        x_m1 = xpad[pl.ds(r0 + halo - 1, tc), :]
        x_0 = xpad[pl.ds(r0 + halo, tc), :]
        x_p1 = xpad[pl.ds(r0 + halo + 1, tc), :]
        x_m2 = jnp.where(jnp.logical_and(left_cut, row < 2), 0.0, x_m2)
        x_m1 = jnp.where(jnp.logical_and(left_cut, row < 1), 0.0, x_m1)
        x_p1 = jnp.where(jnp.logical_and(right_cut, row >= tc - 1), 0.0, x_p1)
        u = cb + x_m2 * cw[0:1] + x_m1 * cw[1:2] + x_0 * cw[2:3] + x_p1 * cw[3:4]
        ub = u.astype(BF16)
        gr = jnp.dot(ub, gw_ref[d, 0].astype(BF16), preferred_element_type=F32) + gb_ref[d, 0]
        gi = jnp.dot(ub, gw_ref[d, 1].astype(BF16), preferred_element_type=F32) + gb_ref[d, 1]
        r = jax.nn.sigmoid(gr)
        i = jax.nn.sigmoid(gi)
        log_a = (-LRU_C * jax.nn.softplus(-lam_ref[d])) * r
        a = jnp.exp(log_a)
        b = jnp.sqrt(-jnp.tanh(log_a) * (a * a + 1.0)) * (i * u)
        return a, b

    def chunk_scan(a, b, h, rev):
        A = a.reshape(nb, SUBLANES, LANES)
        B = b.reshape(nb, SUBLANES, LANES)
        for s in (1, 2, 4):
            shift = SUBLANES - s if rev else s
            msk = (sub < SUBLANES - s) if rev else (sub >= s)
            Ap = pltpu.roll(A, shift, 1)
            Bp = pltpu.roll(B, shift, 1)
            B = jnp.where(msk, A * Bp + B, B)
            A = jnp.where(msk, A * Ap, A)
        outs = [None] * nb
        order = range(nb - 1, -1, -1) if rev else range(nb)
        for jb in order:
            hb = A[jb] * h + B[jb]
            outs[jb] = hb
            edge = hb[0:1, :] if rev else hb[SUBLANES - 1:SUBLANES, :]
            h = jnp.broadcast_to(edge, (SUBLANES, LANES))
        return jnp.concatenate(outs, axis=0), h

    def fwd_body(j, h):
        a, b = coeffs(j, 0)
        hc, h = chunk_scan(a, b, h, False)
        hf[pl.ds(pl.multiple_of(j * tc, tc), tc), :] = hc
        return h

    lax.fori_loop(0, nchunk, fwd_body, jnp.zeros((SUBLANES, LANES), F32))

    def bwd_body(s, h):
        j = jnp.where(s < mc, mc - 1 - s, nchunk - 1 - (s - mc))
        a, b = coeffs(j, 1)
        hc, h = chunk_scan(a, b, h, True)
        r0 = pl.multiple_of(j * tc, tc)
        by = by_ref[pl.ds(r0, tc), :].astype(F32)
        o_ref[pl.ds(r0, tc), :] = ((hf[pl.ds(r0, tc), :] + hc) * jax.nn.gelu(by)).astype(o_ref.dtype)
        return h

    lax.fori_loop(0, nchunk, bwd_body, jnp.zeros((SUBLANES, LANES), F32))


def _rglru(z, conv_w, conv_b, gate_w, gate_b, lru_lambda, m, tc):
    T = z.shape[0]
    nblk = B_WIDTH // B_BW
    return pl.pallas_call(
        functools.partial(_rglru_kernel, m=m, T=T, tc=tc),
        grid=(nblk,),
        in_specs=[
            pl.BlockSpec((T, LANES), lambda c: (0, COL_BX + c)),
            pl.BlockSpec((T, LANES), lambda c: (0, COL_BY + c)),
            pl.BlockSpec((B_CONV, LANES), lambda c: (0, c)),
            pl.BlockSpec((1, LANES), lambda c: (0, c)),
            pl.BlockSpec((2, 2, None, B_BW, B_BW), lambda c: (0, 0, c, 0, 0)),
            pl.BlockSpec((2, 2, None, 1, B_BW), lambda c: (0, 0, c, 0, 0)),
            pl.BlockSpec((2, 1, LANES), lambda c: (0, 0, c)),
        ],
        out_specs=pl.BlockSpec((T, LANES), lambda c: (0, c)),
        out_shape=jax.ShapeDtypeStruct((T, B_WIDTH), BF16),
        scratch_shapes=[pltpu.VMEM((T + 2 * SUBLANES, LANES), F32), pltpu.VMEM((T, LANES), F32)],
        compiler_params=_params(("parallel",)),
        name="rglru",
    )(z, z, conv_w, conv_b.reshape(1, B_WIDTH), gate_w, gate_b.reshape(2, 2, nblk, 1, B_BW),
      lru_lambda.reshape(2, 1, B_WIDTH))


NA_QROWS = 2
NA_KROWS = NA_QROWS + NA_ROWS - 1


def _na_bias(rpb, rows):
    G = rows // NA_QROWS
    reps = [0, 1, 2, G - 2, G - 1]
    q_idx = np.arange(NA_QROWS * GRID_W)
    rho, c = q_idx // GRID_W, q_idx % GRID_W
    k_idx = np.arange(NA_KROWS * GRID_W)
    jj, kc = k_idx // GRID_W, k_idx % GRID_W
    c0 = np.clip(c - NA_COLS // 2, 0, GRID_W - NA_COLS)
    col_ok = (kc[None, :] >= c0[:, None]) & (kc[None, :] < c0[:, None] + NA_COLS)
    col_rel = np.clip(kc[None, :] - c[:, None] + NA_COLS - 1, 0, 2 * NA_COLS - 2)
    rr, ok = [], []
    for g in reps:
        base = int(np.clip(NA_QROWS * g - NA_ROWS // 2, 0, rows - NA_KROWS))
        r = NA_QROWS * g + rho
        r0 = np.clip(r - NA_ROWS // 2, 0, rows - NA_ROWS)
        krow = base + jj
        row_ok = (krow[None, :] >= r0[:, None]) & (krow[None, :] < r0[:, None] + NA_ROWS)
        rr.append(np.clip(krow[None, :] - r[:, None] + NA_ROWS - 1, 0, 2 * NA_ROWS - 2))
        ok.append(row_ok & col_ok)
    rr = np.stack(rr)
    ok = np.stack(ok)
    cc = np.broadcast_to(col_rel[None], rr.shape)
    vals = rpb.astype(F32)[:, rr, cc]
    return jnp.transpose(jnp.where(jnp.asarray(ok)[None], vals, NEG_BIG), (1, 0, 2, 3))


def _na_kernel(q_ref, k_ref, v_ref, bias_ref, o_ref, *, m, rows):
    j = pl.program_id(1)
    tq = q_ref.shape[0]
    qb = q_ref[...] * jnp.asarray(C_DH ** -0.5, BF16)
    lane = lax.broadcasted_iota(jnp.int32, qb.shape, 1)
    zero = jnp.zeros_like(qb)
    head_mask = (lane < C_DH, lane >= C_DH)
    k_cx = k_ref[0:m, :]
    v_cx = v_ref[0:m, :]
    nt = (((1,), (1,)), ((), ()))

    def softmax_pv(parts):
        mx = None
        for s, _ in parts:
            pm = jnp.max(s, axis=-1, keepdims=True)
            mx = pm if mx is None else jnp.maximum(mx, pm)
        l = None
        o = None
        for s, v in parts:
            p = jnp.exp(s - mx)
            ps = jnp.sum(p, axis=-1, keepdims=True)
            po = jnp.dot(p.astype(BF16), v, preferred_element_type=F32)
            l = ps if l is None else l + ps
            o = po if o is None else o + po
        return o / l

    @pl.when(j < m // tq)
    def _():
        outs = []
        for hh in range(2):
            qh = jnp.where(head_mask[hh], qb, zero)
            outs.append(softmax_pv([(lax.dot_general(qh, k_cx, nt, preferred_element_type=F32), v_cx)]))
        o_ref[...] = jnp.where(lane < C_DH, outs[0], outs[1]).astype(o_ref.dtype)

    @pl.when(j >= m // tq)
    def _():
        g = j - m // tq
        base = jnp.clip(NA_QROWS * g - NA_ROWS // 2, 0, rows - NA_KROWS)
        off = pl.multiple_of(m + base * GRID_W, GRID_W)
        k_nb = k_ref[pl.ds(off, NA_KROWS * GRID_W), :]
        v_nb = v_ref[pl.ds(off, NA_KROWS * GRID_W), :]
        outs = []
        for hh in range(2):
            qh = jnp.where(head_mask[hh], qb, zero)
            s_nb = lax.dot_general(qh, k_nb, nt, preferred_element_type=F32) + bias_ref[hh]
            s_cx = lax.dot_general(qh, k_cx, nt, preferred_element_type=F32)
            outs.append(softmax_pv([(s_nb, v_nb), (s_cx, v_cx)]))
        o_ref[...] = jnp.where(lane < C_DH, outs[0], outs[1]).astype(o_ref.dtype)


def _na(z, bias, m):
    T = z.shape[0]
    rows = (T - m) // GRID_W
    tq = NA_QROWS * GRID_W
    G = rows // NA_QROWS
    mq = m // tq

    def pat(h, j):
        g = jnp.maximum(j - mq, 0)
        return jnp.where(g < 2, g, jnp.where(g <= G - 3, 2, g - (G - 5)))

    return pl.pallas_call(
        functools.partial(_na_kernel, m=m, rows=rows),
        grid=(C_HEADS // 2, T // tq),
        in_specs=[
            pl.BlockSpec((tq, LANES), lambda h, j: (j, COL_CQ + h)),
            pl.BlockSpec((T, LANES), lambda h, j: (0, COL_CK + h)),
            pl.BlockSpec((T, LANES), lambda h, j: (0, COL_CV + h)),
            pl.BlockSpec((None, 2, tq, NA_KROWS * GRID_W), lambda h, j: (pat(h, j), h, 0, 0)),
        ],
        out_specs=pl.BlockSpec((tq, LANES), lambda h, j: (j, h)),
        out_shape=jax.ShapeDtypeStruct((T, C_HEADS * C_DH), BF16),
        compiler_params=_params(("parallel", "arbitrary")),
        name="nbr_attn",
    )(z, z, z, bias)


def _merge_kernel(oa_ref, ob_ref, oc_ref, wb_ref, g0_ref, g1_ref, g2_ref, o_ref):
    acc = None
    for i, (o, g) in enumerate(((oa_ref, g0_ref), (ob_ref, g1_ref), (oc_ref, g2_ref))):
        t = jax.nn.sigmoid(g[...].astype(F32)) * jnp.dot(o[...], wb_ref[i], preferred_element_type=F32)
        acc = t if acc is None else acc + t
    o_ref[...] = acc.astype(o_ref.dtype)


def _merge(oa, ob, oc, wb, z, D, tm, tn):
    T = oa.shape[0]
    gcol = COL_G * LANES // tn
    nd = D // tn
    o_spec = pl.BlockSpec((tm, BRANCH_W), lambda i, j: (i, 0))
    return pl.pallas_call(
        _merge_kernel,
        grid=(T // tm, D // tn),
        in_specs=[o_spec, o_spec, o_spec,
                  pl.BlockSpec((N_BRANCH, BRANCH_W, tn), lambda i, j: (0, 0, j)),
                  pl.BlockSpec((tm, tn), lambda i, j: (i, gcol + j)),
                  pl.BlockSpec((tm, tn), lambda i, j: (i, gcol + nd + j)),
                  pl.BlockSpec((tm, tn), lambda i, j: (i, gcol + 2 * nd + j))],
        out_specs=pl.BlockSpec((tm, tn), lambda i, j: (i, j)),
        out_shape=jax.ShapeDtypeStruct((T, D), BF16),
        compiler_params=_params(("parallel", "arbitrary")),
        name="merge",
    )(oa, ob, oc, wb, z, z, z)


def _outproj_kernel(y_ref, w_ref, x_ref, gate_ref, o_ref, *, m):
    tm = y_ref.shape[0]
    r = jnp.dot(y_ref[...], w_ref[...], preferred_element_type=F32)
    row = pl.program_id(0) * tm + lax.broadcasted_iota(jnp.int32, r.shape, 0)
    g = jnp.where(row < m, gate_ref[1:2, :], gate_ref[0:1, :])
    o_ref[...] = x_ref[...] + g * r


def _outproj(y, w, xs, mod, gate_idx, m, tm, tn):
    T, D = xs.shape
    nd = D // tn
    return pl.pallas_call(
        functools.partial(_outproj_kernel, m=m),
        grid=(T // tm, nd),
        in_specs=[pl.BlockSpec((tm, D), lambda i, j: (i, 0)),
                  pl.BlockSpec((D, tn), lambda i, j: (0, j)),
                  pl.BlockSpec((tm, tn), lambda i, j: (i, j)),
                  pl.BlockSpec((SUBLANES, tn), lambda i, j: (0, gate_idx * nd + j))],
        out_specs=pl.BlockSpec((tm, tn), lambda i, j: (i, j)),
        out_shape=jax.ShapeDtypeStruct((T, D), F32),
        compiler_params=_params(("parallel", "arbitrary")),
        name="out_proj",
    )(y, w, xs, mod)


HI_MASK = 0xFFFF0000


def _pack_rows(v):
    half = v.shape[1] // 2
    u = lax.bitcast_convert_type(v.astype(BF16).astype(F32), jnp.uint32)
    return (u[:, half:] & jnp.uint32(HI_MASK)) | (u[:, :half] >> 16)


def _unpack_rows(w):
    lo = lax.bitcast_convert_type(w << 16, F32)
    hi = lax.bitcast_convert_type(w & jnp.uint32(HI_MASK), F32)
    return lo, hi

def _router_kernel(x_ref, g_ref, sh_ref, sc_ref, wr_ref, br_ref, h_ref, idx_ref, gate_ref, rank_ref, cnt_ref,
                   run_ref, *, m_tiles, n_experts):
    i = pl.program_id(0)
    is_ctx = i < m_tiles

    @pl.when(i == 0)
    def _():
        run_ref[...] = jnp.zeros_like(run_ref)

    x = x_ref[...]
    tr = x.shape[0]
    y = x * lax.rsqrt(jnp.mean(x * x, axis=-1, keepdims=True) + NORM_EPS) * g_ref[...]
    sh = jnp.where(is_ctx, sh_ref[1:2, :], sh_ref[0:1, :])
    sc = jnp.where(is_ctx, sc_ref[1:2, :], sc_ref[0:1, :])
    h = y * (1.0 + sc) + sh
    hb = h.astype(BF16)
    h_ref[...] = _pack_rows(h)

    h_lo = (h - hb.astype(F32)).astype(BF16)
    w = wr_ref[...]
    w_hi = w.astype(BF16)
    w_lo = (w - w_hi.astype(F32)).astype(BF16)
    logits = (jnp.dot(hb, w_hi, preferred_element_type=F32) + jnp.dot(h_lo, w_hi, preferred_element_type=F32)
              + jnp.dot(hb, w_lo, preferred_element_type=F32)) + br_ref[...]
    lane = lax.broadcasted_iota(jnp.int32, logits.shape, 1)
    work = jnp.where(lane < n_experts, logits, NEG_BIG)

    vals, idxs = [], []
    for _ in range(TOP_K):
        mx = jnp.max(work, axis=-1, keepdims=True)
        ix = jnp.min(jnp.where(work == mx, lane, LANES), axis=-1, keepdims=True)
        vals.append(mx)
        idxs.append(ix)
        work = jnp.where(lane == ix, NEG_BIG, work)
    es = [jnp.exp(v - vals[0]) for v in vals]
    den = es[0] + es[1] + es[2] + es[3]

    onehot = jnp.zeros(logits.shape, F32)
    for ix in idxs:
        onehot = onehot + jnp.where(lane == ix, 1.0, 0.0)
    r_i = lax.broadcasted_iota(jnp.int32, (tr, tr), 0)
    c_i = lax.broadcasted_iota(jnp.int32, (tr, tr), 1)
    tri = jnp.where(c_i < r_i, 1.0, 0.0).astype(BF16)
    before = jnp.dot(tri, onehot.astype(BF16), preferred_element_type=F32) + run_ref[...]

    idx_o = jnp.zeros(logits.shape, jnp.int32)
    gate_o = jnp.zeros(logits.shape, F32)
    rank_o = jnp.zeros(logits.shape, jnp.int32)
    for k in range(TOP_K):
        rk = jnp.sum(jnp.where(lane == idxs[k], before, 0.0), axis=-1, keepdims=True)
        idx_o = jnp.where(lane == k, idxs[k], idx_o)
        gate_o = jnp.where(lane == k, es[k] / den, gate_o)
        rank_o = jnp.where(lane == k, rk.astype(jnp.int32), rank_o)
    idx_ref[...] = idx_o
    gate_ref[...] = gate_o
    rank_ref[...] = rank_o
    run_ref[...] += jnp.sum(onehot, axis=0, keepdims=True)
    cnt_ref[...] = jnp.broadcast_to(run_ref[...], cnt_ref.shape)


def _router(xs, g, mod, shift_idx, scale_idx, w_router, b_router, m, tr):
    T, D = xs.shape
    E = w_router.shape[1]
    wr = jnp.pad(w_router, ((0, 0), (0, LANES - E)))
    br = jnp.pad(b_router, (0, LANES - E)).reshape(1, LANES)
    slab = pl.BlockSpec((tr, LANES), lambda i: (i, 0))
    return pl.pallas_call(
        functools.partial(_router_kernel, m_tiles=m // tr, n_experts=E),
        grid=(T // tr,),
        in_specs=[
            pl.BlockSpec((tr, D), lambda i: (i, 0)),
            pl.BlockSpec((1, D), lambda i: (0, 0)),
            pl.BlockSpec((SUBLANES, D), lambda i: (0, shift_idx)),
            pl.BlockSpec((SUBLANES, D), lambda i: (0, scale_idx)),
            pl.BlockSpec((D, LANES), lambda i: (0, 0)),
            pl.BlockSpec((1, LANES), lambda i: (0, 0)),
        ],
        out_specs=[pl.BlockSpec((tr, D // 2), lambda i: (i, 0)), slab, slab, slab,
                   pl.BlockSpec((SUBLANES, LANES), lambda i: (0, 0))],
        out_shape=[jax.ShapeDtypeStruct((T, D // 2), jnp.uint32), jax.ShapeDtypeStruct((T, LANES), jnp.int32),
                   jax.ShapeDtypeStruct((T, LANES), F32), jax.ShapeDtypeStruct((T, LANES), jnp.int32),
                   jax.ShapeDtypeStruct((SUBLANES, LANES), F32)],
        scratch_shapes=[pltpu.VMEM((1, LANES), F32)],
        compiler_params=_params(("arbitrary",)),
        name="router",
    )(xs, g.reshape(1, D), mod, mod, wr, br)


def _dispatch_kernel(dest_ref, h_ref, xs_in_ref, xs_ref, sem):
    del xs_in_ref
    tr = h_ref.shape[0]

    def copy(r, k):
        return pltpu.make_async_copy(h_ref.at[pl.ds(r, 1), :], xs_ref.at[pl.ds(dest_ref[0, r * TOP_K + k], 1), :], sem)

    def issue(r, c):
        for k in range(TOP_K):
            copy(r, k).start()
        return c

    lax.fori_loop(0, tr, issue, 0)

    def drain(r, c):
        for k in range(TOP_K):
            copy(r, k).wait()
        return c

    lax.fori_loop(0, tr, drain, 0)


def _dispatch(h, dest, n_rows, tr):
    T, D = h.shape
    xs0 = jnp.zeros((n_rows, D), h.dtype)
    return pl.pallas_call(
        _dispatch_kernel,
        grid=(T // tr,),
        in_specs=[
            pl.BlockSpec((None, 1, tr * TOP_K), lambda i: (i, 0, 0), memory_space=pltpu.SMEM),
            pl.BlockSpec((tr, D), lambda i: (i, 0)),
            pl.BlockSpec(memory_space=pl.ANY),
        ],
        out_specs=pl.BlockSpec(memory_space=pl.ANY),
        out_shape=jax.ShapeDtypeStruct((n_rows, D), h.dtype),
        scratch_shapes=[pltpu.SemaphoreType.DMA(())],
        input_output_aliases={2: 0},
        compiler_params=_params(("arbitrary",)),
        name="dispatch",
    )(dest.reshape(T // tr, 1, tr * TOP_K), h, xs0)


def _expert_kernel(be_ref, nu_ref, x_ref, wg_ref, wu_ref, bg_ref, bu_ref, wd_ref, bd_ref, o_ref):
    i = pl.program_id(0)

    @pl.when(i < nu_ref[0])
    def _():
        x_lo, x_hi = _unpack_rows(x_ref[...])
        x = jnp.concatenate([x_lo.astype(BF16), x_hi.astype(BF16)], axis=1)
        glu = jnp.minimum(jnp.dot(x, wg_ref[...], preferred_element_type=F32) + bg_ref[...], SWIGLU_LIMIT)
        lin = jnp.clip(jnp.dot(x, wu_ref[...], preferred_element_type=F32) + bu_ref[...], -SWIGLU_LIMIT, SWIGLU_LIMIT)
        act = glu * jax.nn.sigmoid(SWIGLU_ALPHA * glu) * (lin + 1.0)
        y = jnp.dot(act.astype(BF16), wd_ref[...], preferred_element_type=F32) + bd_ref[...]
        o_ref[...] = _pack_rows(y)

    @pl.when(i >= nu_ref[0])
    def _():
        o_ref[...] = jnp.zeros_like(o_ref)


def _experts(xs, block_e, n_used, wg, wu, bg, bu, wd, bd, bm):
    n_rows, Dw = xs.shape
    E, D, F = wg.shape
    grid_spec = pltpu.PrefetchScalarGridSpec(
        num_scalar_prefetch=2,
        grid=(n_rows // bm,),
        in_specs=[
            pl.BlockSpec((bm, Dw), lambda i, be, nu: (i, 0)),
            pl.BlockSpec((None, D, F), lambda i, be, nu: (be[i], 0, 0)),
            pl.BlockSpec((None, D, F), lambda i, be, nu: (be[i], 0, 0)),
            pl.BlockSpec((None, 1, F), lambda i, be, nu: (be[i], 0, 0)),
            pl.BlockSpec((None, 1, F), lambda i, be, nu: (be[i], 0, 0)),
            pl.BlockSpec((None, F, D), lambda i, be, nu: (be[i], 0, 0)),
            pl.BlockSpec((None, 1, D), lambda i, be, nu: (be[i], 0, 0)),
        ],
        out_specs=pl.BlockSpec((bm, Dw), lambda i, be, nu: (i, 0)),
    )
    return pl.pallas_call(
        _expert_kernel,
        grid_spec=grid_spec,
        out_shape=jax.ShapeDtypeStruct((n_rows, Dw), jnp.uint32),
        compiler_params=_params(("arbitrary",)),
        name="experts",
    )(block_e, n_used, xs, wg, wu, bg.reshape(E, 1, F), bu.reshape(E, 1, F), wd, bd.reshape(E, 1, D))


def _combine_kernel(dest_ref, ys_ref, gatew_ref, x_ref, mod_ref, o_ref, buf, sem, *, m_tiles):
    tr = x_ref.shape[0]
    is_ctx = pl.program_id(0) < m_tiles

    def copy(r, k):
        return pltpu.make_async_copy(ys_ref.at[pl.ds(dest_ref[0, r * TOP_K + k], 1), :], buf.at[k, pl.ds(r, 1), :], sem)

    def issue(r, c):
        for k in range(TOP_K):
            copy(r, k).start()
        return c

    lax.fori_loop(0, tr, issue, 0)

    def drain(r, c):
        for k in range(TOP_K):
            copy(r, k).wait()
        return c

    lax.fori_loop(0, tr, drain, 0)

    gw = gatew_ref[...]
    f_lo = f_hi = None
    for k in range(TOP_K):
        lo, hi = _unpack_rows(buf[k])
        f_lo = gw[:, k:k + 1] * lo if f_lo is None else f_lo + gw[:, k:k + 1] * lo
        f_hi = gw[:, k:k + 1] * hi if f_hi is None else f_hi + gw[:, k:k + 1] * hi
    g = jnp.where(is_ctx, mod_ref[1:2, :], mod_ref[0:1, :])
    o_ref[...] = x_ref[...] + g * jnp.concatenate([f_lo, f_hi], axis=1)


def _combine(ys, dest, gatew, xs, mod, gate_idx, m, tr):
    T, D = xs.shape
    return pl.pallas_call(
        functools.partial(_combine_kernel, m_tiles=m // tr),
        grid=(T // tr,),
        in_specs=[
            pl.BlockSpec((None, 1, tr * TOP_K), lambda i: (i, 0, 0), memory_space=pltpu.SMEM),
            pl.BlockSpec(memory_space=pl.ANY),
            pl.BlockSpec((tr, LANES), lambda i: (i, 0)),
            pl.BlockSpec((tr, D), lambda i: (i, 0)),
            pl.BlockSpec((SUBLANES, D), lambda i: (0, gate_idx)),
        ],
        out_specs=pl.BlockSpec((tr, D), lambda i: (i, 0)),
        out_shape=jax.ShapeDtypeStruct((T, D), F32),
        scratch_shapes=[pltpu.VMEM((TOP_K, tr, D // 2), jnp.uint32), pltpu.SemaphoreType.DMA(())],
        compiler_params=_params(("arbitrary",)),
        name="combine",
    )(dest.reshape(T // tr, 1, tr * TOP_K), ys, gatew, xs, mod)


def _final_norm_kernel(x_ref, g_ref, o_ref):
    x = x_ref[...]
    o_ref[...] = x * lax.rsqrt(jnp.mean(x * x, axis=-1, keepdims=True) + NORM_EPS) * g_ref[...]


def _final_norm(xs, g, m, n, tr):
    D = xs.shape[1]
    off = m // tr
    return pl.pallas_call(
        _final_norm_kernel,
        grid=(n // tr,),
        in_specs=[pl.BlockSpec((tr, D), lambda i: (i + off, 0)), pl.BlockSpec((1, D), lambda i: (0, 0))],
        out_specs=pl.BlockSpec((tr, D), lambda i: (i, 0)),
        out_shape=jax.ShapeDtypeStruct((n, D), F32),
        compiler_params=_params(("parallel",)),
        name="final_norm",
    )(xs, g.reshape(1, D))


def kernel(x, c, ctx, c_ctx, w_ada, b_ada, norm1_g, norm2_g, w_in, lam_qk, subln_g, conv_w, conv_b, lru_gate_w,
           lru_gate_b, lru_lambda, na_rpb, w_branch, w_out, w_router, b_router, w_gu, b_gu, w_down, b_down, final_g):
    B, n, D = x.shape
    m = ctx.shape[1]
    assert B == 1 and n % (NA_QROWS * GRID_W) == 0
    L = w_ada.shape[0]
    E = w_router.shape[-1]
    T = m + n
    rows = n // GRID_W
    tr = 256
    assert m % tr == 0 and T % tr == 0 and m % (NA_QROWS * GRID_W) == 0
    tm = _tile(T, 1408, 128)
    tn = _tile(D, 512, LANES)
    bm = 256
    n_blocks = -(-(T * TOP_K + E * (bm - 1)) // bm)
    n_rows = n_blocks * bm

    xs = jnp.concatenate([ctx[0], x[0]], axis=0)
    mod_all = _ada(c, c_ctx, w_ada, b_ada)
    cos, sin = _rope_tables(m, n)

    for l in range(L):
        lambda_init = 0.8 - 0.6 * math.exp(-0.3 * l)
        mod = mod_all[l]
        h = _norm_mod(xs, norm1_g[l], mod, 0, 1, m, tr)
        z = _matmul(h, w_in[l].astype(BF16), tm, tn, BF16)
        qk = _rope(z, cos, sin, tr)
        o_a = _attn_a(qk, z, lam_qk[l], subln_g[l], lambda_init, m, 256, _tile(T, 1024, 256))
        o_b = _rglru(z, conv_w[l], conv_b[l], lru_gate_w[l], lru_gate_b[l], lru_lambda[l], m, 256)
        o_c = _na(z, _na_bias(na_rpb[l], rows), m)
        y = _merge(o_a, o_b, o_c, w_branch[l].astype(BF16), z, D, tm, tn)
        xs = _outproj(y, w_out[l].astype(BF16), xs, mod, 2, m, tm, tn)

        h2, idx, gatew, rank, cnt = _router(xs, norm2_g[l], mod, 3, 4, w_router[l], b_router[l], m, tr)
        counts = cnt[0, :E].astype(jnp.int32)
        padded = (counts + bm - 1) // bm * bm
        pend = jnp.cumsum(padded)
        pstart = pend - padded
        dest = (pstart[idx[:, :TOP_K]] + rank[:, :TOP_K]).astype(jnp.int32)
        block_e = jnp.minimum(jnp.searchsorted(pend, jnp.arange(n_blocks) * bm, side='right'), E - 1).astype(jnp.int32)
        n_used = (pend[-1:] // bm).astype(jnp.int32)

        xg = _dispatch(h2, dest, n_rows, tr)
        wgu = w_gu[l].astype(BF16)
        ys = _experts(xg, block_e, n_used, wgu[:, :, 0::2], wgu[:, :, 1::2], b_gu[l][:, 0::2], b_gu[l][:, 1::2],
                      w_down[l].astype(BF16), b_down[l], bm)
        xs = _combine(ys, dest, gatew, xs, mod, 5, m, tr)

    return _final_norm(xs, final_g, m, n, tr)[None]
```

```python
import functools
import math

import numpy as np
import jax
import jax.numpy as jnp
from jax import lax
from jax.experimental import pallas as pl
from jax.experimental.pallas import tpu as pltpu

F32 = jnp.float32
BF16 = jnp.bfloat16

GRID_W = 64
NORM_EPS = 1e-6
A_HEADS, A_DH = 8, 64
ROPE_PAIRS = A_DH // 4
ROPE_BASE = 10000.0
B_WIDTH, B_BW, B_CONV, CONV_LEFT, LRU_C = 1024, 128, 4, 2, 8.0
C_HEADS, C_DH, NA_ROWS, NA_COLS = 16, 64, 8, 16
N_BRANCH, BRANCH_W = 3, 1024
TOP_K = 4
SWIGLU_ALPHA, SWIGLU_LIMIT = 1.702, 7.0

LANES = 128
SUBLANES = 8
COL_AQ, COL_AK, COL_AV, COL_BX, COL_BY, COL_CQ, COL_CK, COL_CV, COL_G = 0, 8, 16, 24, 32, 40, 48, 56, 64
NEG_BIG = -1e30
VMEM_LIMIT = 56 * 1024 * 1024


def _params(sem):
    return pltpu.CompilerParams(dimension_semantics=sem, vmem_limit_bytes=VMEM_LIMIT)


def _tile(dim, target, mult):
    best = None
    for t in range(mult, min(dim, target) + 1, mult):
        if dim % t == 0:
            best = t
    assert best is not None, (dim, target, mult)
    return best


def _ada_kernel(c_ref, w_ref, b_ref, o_ref, acc_ref):
    k = pl.program_id(2)

    @pl.when(k == 0)
    def _():
        acc_ref[...] = jnp.zeros_like(acc_ref)

    cc = c_ref[...]
    s = cc * jax.nn.sigmoid(cc)
    hi = s.astype(BF16)
    lo = (s - hi.astype(F32)).astype(BF16)
    row = lax.broadcasted_iota(jnp.int32, s.shape, 0)
    lhs = jnp.where(row < 2, hi, lo)
    acc_ref[...] += jnp.dot(lhs, w_ref[...].astype(BF16), preferred_element_type=F32)

    @pl.when(k == pl.num_programs(2) - 1)
    def _():
        acc = acc_ref[...]
        o_ref[...] = acc + pltpu.roll(acc, SUBLANES - 2, 0) + b_ref[...]


def _ada(c, c_ctx, w_ada, b_ada):
    L, D, N = w_ada.shape
    cc = jnp.concatenate([c[:1], c_ctx[None], c[:1], c_ctx[None], jnp.zeros((4, D), F32)], axis=0)
    tk = _tile(D, 1024, LANES)
    tn = _tile(N, 2048, LANES)
    return pl.pallas_call(
        _ada_kernel,
        grid=(L, N // tn, D // tk),
        in_specs=[
            pl.BlockSpec((SUBLANES, tk), lambda l, j, k: (0, k)),
            pl.BlockSpec((None, tk, tn), lambda l, j, k: (l, k, j)),
            pl.BlockSpec((None, 1, tn), lambda l, j, k: (l, 0, j)),
        ],
        out_specs=pl.BlockSpec((None, SUBLANES, tn), lambda l, j, k: (l, 0, j)),
        out_shape=jax.ShapeDtypeStruct((L, SUBLANES, N), F32),
        scratch_shapes=[pltpu.VMEM((SUBLANES, tn), F32)],
        compiler_params=_params(("parallel", "parallel", "arbitrary")),
        name="ada",
    )(cc, w_ada, b_ada.reshape(L, 1, N))


def _norm_mod_kernel(x_ref, g_ref, sh_ref, sc_ref, o_ref, *, m_tiles):
    is_ctx = pl.program_id(0) < m_tiles
    x = x_ref[...]
    y = x * lax.rsqrt(jnp.mean(x * x, axis=-1, keepdims=True) + NORM_EPS) * g_ref[...]
    sh = jnp.where(is_ctx, sh_ref[1:2, :], sh_ref[0:1, :])
    sc = jnp.where(is_ctx, sc_ref[1:2, :], sc_ref[0:1, :])
    o_ref[...] = (y * (1.0 + sc) + sh).astype(o_ref.dtype)


def _norm_mod(xs, g, mod, shift_idx, scale_idx, m, tr):
    T, D = xs.shape
    return pl.pallas_call(
        functools.partial(_norm_mod_kernel, m_tiles=m // tr),
        grid=(T // tr,),
        in_specs=[
            pl.BlockSpec((tr, D), lambda i: (i, 0)),
            pl.BlockSpec((1, D), lambda i: (0, 0)),
            pl.BlockSpec((SUBLANES, D), lambda i: (0, shift_idx)),
            pl.BlockSpec((SUBLANES, D), lambda i: (0, scale_idx)),
        ],
        out_specs=pl.BlockSpec((tr, D), lambda i: (i, 0)),
        out_shape=jax.ShapeDtypeStruct((T, D), BF16),
        compiler_params=_params(("parallel",)),
        name="norm_mod",
    )(xs, g.reshape(1, D), mod, mod)


def _mm_kernel(a_ref, b_ref, o_ref):
    o_ref[...] = jnp.dot(a_ref[...], b_ref[...], preferred_element_type=F32).astype(o_ref.dtype)


def _matmul(a, b, tm, tn, out_dtype):
    M, K = a.shape
    _, N = b.shape
    return pl.pallas_call(
        _mm_kernel,
        grid=(M // tm, N // tn),
        in_specs=[pl.BlockSpec((tm, K), lambda i, j: (i, 0)), pl.BlockSpec((K, tn), lambda i, j: (0, j))],
        out_specs=pl.BlockSpec((tm, tn), lambda i, j: (i, j)),
        out_shape=jax.ShapeDtypeStruct((M, N), out_dtype),
        compiler_params=_params(("parallel", "arbitrary")),
        name="in_proj",
    )(a, b)


def _rope_tables(m, n):
    t = np.arange(n)
    row = (t // GRID_W).astype(np.float32)
    col = (t % GRID_W).astype(np.float32)
    inv = jnp.asarray(ROPE_BASE, F32) ** (-jnp.arange(ROPE_PAIRS, dtype=F32) / ROPE_PAIRS)
    ang_r = jnp.asarray(row)[:, None] * inv
    ang_c = jnp.asarray(col)[:, None] * inv
    lane = np.arange(LANES)
    d = lane % A_DH
    use_col = (d // (A_DH // 2)) == 1
    within = d % (A_DH // 2)
    pair = within % ROPE_PAIRS
    sign = np.where(within < ROPE_PAIRS, -1.0, 1.0).astype(np.float32)
    ang = jnp.where(jnp.asarray(use_col)[None, :], ang_c[:, pair], ang_r[:, pair])
    cos = jnp.concatenate([jnp.ones((m, LANES), F32), jnp.cos(ang)], axis=0)
    sin = jnp.concatenate([jnp.zeros((m, LANES), F32), jnp.sin(ang) * jnp.asarray(sign)[None, :]], axis=0)
    return cos, sin


def _rope_kernel(z_ref, cos_ref, sin_ref, o_ref):
    scale = jnp.where(pl.program_id(1) == 0, A_DH ** -0.5, 1.0).astype(F32)
    cos = cos_ref[...]
    sin = sin_ref[...]
    lane = lax.broadcasted_iota(jnp.int32, cos.shape, 1)
    lo = (lane % (2 * ROPE_PAIRS)) < ROPE_PAIRS
    for hc in range(z_ref.shape[1] // LANES):
        x = z_ref[:, hc * LANES:(hc + 1) * LANES].astype(F32)
        partner = jnp.where(lo, pltpu.roll(x, LANES - ROPE_PAIRS, 1), pltpu.roll(x, ROPE_PAIRS, 1))
        o_ref[:, hc * LANES:(hc + 1) * LANES] = ((x * cos + partner * sin) * scale).astype(o_ref.dtype)


def _rope(z, cos, sin, tr):
    T = z.shape[0]
    W = A_HEADS * 2 * A_DH
    return pl.pallas_call(
        _rope_kernel,
        grid=(T // tr, 2),
        in_specs=[
            pl.BlockSpec((tr, W), lambda i, j: (i, j)),
            pl.BlockSpec((tr, LANES), lambda i, j: (i, 0)),
            pl.BlockSpec((tr, LANES), lambda i, j: (i, 0)),
        ],
        out_specs=pl.BlockSpec((tr, W), lambda i, j: (i, j)),
        out_shape=jax.ShapeDtypeStruct((T, 2 * W), BF16),
        compiler_params=_params(("parallel", "arbitrary")),
        name="rope",
    )(z, cos, sin)


def _attn_a_kernel(lam_ref, g_ref, q_ref, k_ref, v_ref, o_ref, *, m, T, tk, lambda_init):
    qi = pl.program_id(1)
    qb = q_ref[...]
    tq = qb.shape[0]
    lane = lax.broadcasted_iota(jnp.int32, qb.shape, 1)
    zero = jnp.zeros_like(qb)
    qs = jnp.concatenate([jnp.where(lane < A_DH, qb, zero), jnp.where(lane >= A_DH, qb, zero)], axis=0)

    def scores(kc):
        return lax.dot_general(qs, kc, (((1,), (1,)), ((), ())), preferred_element_type=F32)

    def step(s, vc, carry):
        m_i, l_i, acc = carry
        m_new = jnp.maximum(m_i, jnp.max(s, axis=-1, keepdims=True))
        alpha = jnp.exp(m_i - m_new)
        p = jnp.exp(s - m_new)
        l_new = alpha * l_i + jnp.sum(p, axis=-1, keepdims=True)
        acc = alpha * acc + jnp.dot(p.astype(BF16), vc, preferred_element_type=F32)
        return m_new, l_new, acc

    init = (jnp.full((2 * tq, 1), NEG_BIG, F32), jnp.zeros((2 * tq, 1), F32), jnp.zeros((2 * tq, LANES), F32))

    lq = lam_ref[...]
    lam = (jnp.exp(jnp.sum(lq[0:1] * lq[1:2], axis=-1, keepdims=True))
           - jnp.exp(jnp.sum(lq[2:3] * lq[3:4], axis=-1, keepdims=True)) + lambda_init)

    def finish(carry):
        _, l_i, acc = carry
        o = acc / l_i
        d = o[:tq] - lam * o[tq:]
        y = d * lax.rsqrt(jnp.mean(d * d, axis=-1, keepdims=True) + NORM_EPS) * g_ref[...]
        o_ref[...] = (y * (1.0 - lambda_init)).astype(o_ref.dtype)

    @pl.when(qi < m // tq)
    def _():
        finish(step(scores(k_ref[0:m, :]), v_ref[0:m, :], init))

    @pl.when(qi >= m // tq)
    def _():
        bounds = [(lo, min(lo + tk, T)) for lo in range(0, T, tk)]
        carry = init
        s = scores(k_ref[bounds[0][0]:bounds[0][1], :])
        for c, (lo, hi) in enumerate(bounds):
            s_next = scores(k_ref[bounds[c + 1][0]:bounds[c + 1][1], :]) if c + 1 < len(bounds) else None
            carry = step(s, v_ref[lo:hi, :], carry)
            s = s_next
        finish(carry)


def _attn_a(qk, z, lam_qk, subln_g, lambda_init, m, tq, tk):
    T = z.shape[0]
    return pl.pallas_call(
        functools.partial(_attn_a_kernel, m=m, T=T, tk=tk, lambda_init=lambda_init),
        grid=(A_HEADS, T // tq),
        in_specs=[
            pl.BlockSpec((4, A_DH), lambda h, i: (0, 0)),
            pl.BlockSpec((1, 2 * A_DH), lambda h, i: (0, 0)),
            pl.BlockSpec((tq, LANES), lambda h, i: (i, h)),
            pl.BlockSpec((T, LANES), lambda h, i: (0, A_HEADS + h)),
            pl.BlockSpec((T, LANES), lambda h, i: (0, COL_AV + h)),
        ],
        out_specs=pl.BlockSpec((tq, LANES), lambda h, i: (i, h)),
        out_shape=jax.ShapeDtypeStruct((T, A_HEADS * 2 * A_DH), BF16),
        compiler_params=_params(("parallel", "arbitrary")),
        name="diff_attn",
    )(lam_qk, subln_g.reshape(1, 2 * A_DH), qk, qk, z)


def _rglru_kernel(bx_ref, by_ref, cw_ref, cb_ref, gw_ref, gb_ref, lam_ref, o_ref, xpad, hf, *, m, T, tc):
    nchunk = T // tc
    mc = m // tc
    nb = tc // SUBLANES
    halo = SUBLANES
    xpad[0:halo, :] = jnp.zeros((halo, LANES), F32)
    xpad[halo + T:2 * halo + T, :] = jnp.zeros((halo, LANES), F32)
    xpad[halo:halo + T, :] = bx_ref[...].astype(F32)

    row = lax.broadcasted_iota(jnp.int32, (tc, LANES), 0)
    sub = lax.broadcasted_iota(jnp.int32, (nb, SUBLANES, LANES), 1)
    cw = cw_ref[...]
    cb = cb_ref[...]

    def coeffs(j, d):
        r0 = j * tc
        left_cut = r0 == m
        right_cut = (r0 + tc) == m
        x_m2 = xpad[pl.ds(r0 + halo - 2, tc), :]
        x_m1 = xpad[pl.ds(r0 + halo - 1, tc), :]
        x_0 = xpad[pl.ds(r0 + halo, tc), :]
        x_p1 = xpad[pl.ds(r0 + halo + 1, tc), :]
        x_m2 = jnp.where(jnp.logical_and(left_cut, row < 2), 0.0, x_m2)
        x_m1 = jnp.where(jnp.logical_and(left_cut, row < 1), 0.0, x_m1)
        x_p1 = jnp.where(jnp.logical_and(right_cut, row >= tc - 1), 0.0, x_p1)
        u = cb + x_m2 * cw[0:1] + x_m1 * cw[1:2] + x_0 * cw[2:3] + x_p1 * cw[3:4]
        ub = u.astype(BF16)
        gr = jnp.dot(ub, gw_ref[d, 0].astype(BF16), preferred_element_type=F32) + gb_ref[d, 0]
        gi = jnp.dot(ub, gw_ref[d, 1].astype(BF16), preferred_element_type=F32) + gb_ref[d, 1]
        r = jax.nn.sigmoid(gr)
        i = jax.nn.sigmoid(gi)
        log_a = (-LRU_C * jax.nn.softplus(-lam_ref[d])) * r
        a = jnp.exp(log_a)
        b = jnp.sqrt(-jnp.tanh(log_a) * (a * a + 1.0)) * (i * u)
        return a, b

    def chunk_scan(a, b, h, rev):
        A = a.reshape(nb, SUBLANES, LANES)
        B = b.reshape(nb, SUBLANES, LANES)
        for s in (1, 2, 4):
            shift = SUBLANES - s if rev else s
            msk = (sub < SUBLANES - s) if rev else (sub >= s)
            Ap = pltpu.roll(A, shift, 1)
            Bp = pltpu.roll(B, shift, 1)
            B = jnp.where(msk, A * Bp + B, B)
            A = jnp.where(msk, A * Ap, A)
        outs = [None] * nb
        order = range(nb - 1, -1, -1) if rev else range(nb)
        for jb in order:
            hb = A[jb] * h + B[jb]
            outs[jb] = hb
            edge = hb[0:1, :] if rev else hb[SUBLANES - 1:SUBLANES, :]
            h = jnp.broadcast_to(edge, (SUBLANES, LANES))
        return jnp.concatenate(outs, axis=0), h

    def fwd_body(j, h):
        a, b = coeffs(j, 0)
        hc, h = chunk_scan(a, b, h, False)
        hf[pl.ds(pl.multiple_of(j * tc, tc), tc), :] = hc
        return h

    lax.fori_loop(0, nchunk, fwd_body, jnp.zeros((SUBLANES, LANES), F32))

    def bwd_body(s, h):
        j = jnp.where(s < mc, mc - 1 - s, nchunk - 1 - (s - mc))
        a, b = coeffs(j, 1)
        hc, h = chunk_scan(a, b, h, True)
        r0 = pl.multiple_of(j * tc, tc)
        by = by_ref[pl.ds(r0, tc), :].astype(F32)
        o_ref[pl.ds(r0, tc), :] = ((hf[pl.ds(r0, tc), :] + hc) * jax.nn.gelu(by)).astype(o_ref.dtype)
        return h

    lax.fori_loop(0, nchunk, bwd_body, jnp.zeros((SUBLANES, LANES), F32))


def _rglru(z, conv_w, conv_b, gate_w, gate_b, lru_lambda, m, tc):
    T = z.shape[0]
    nblk = B_WIDTH // B_BW
    return pl.pallas_call(
        functools.partial(_rglru_kernel, m=m, T=T, tc=tc),
        grid=(nblk,),
        in_specs=[
            pl.BlockSpec((T, LANES), lambda c: (0, COL_BX + c)),
            pl.BlockSpec((T, LANES), lambda c: (0, COL_BY + c)),
            pl.BlockSpec((B_CONV, LANES), lambda c: (0, c)),
            pl.BlockSpec((1, LANES), lambda c: (0, c)),
            pl.BlockSpec((2, 2, None, B_BW, B_BW), lambda c: (0, 0, c, 0, 0)),
            pl.BlockSpec((2, 2, None, 1, B_BW), lambda c: (0, 0, c, 0, 0)),
            pl.BlockSpec((2, 1, LANES), lambda c: (0, 0, c)),
        ],
        out_specs=pl.BlockSpec((T, LANES), lambda c: (0, c)),
        out_shape=jax.ShapeDtypeStruct((T, B_WIDTH), BF16),
        scratch_shapes=[pltpu.VMEM((T + 2 * SUBLANES, LANES), F32), pltpu.VMEM((T, LANES), F32)],
        compiler_params=_params(("parallel",)),
        name="rglru",
    )(z, z, conv_w, conv_b.reshape(1, B_WIDTH), gate_w, gate_b.reshape(2, 2, nblk, 1, B_BW),
      lru_lambda.reshape(2, 1, B_WIDTH))


NA_QROWS = 4
NA_KROWS = NA_QROWS + NA_ROWS - 1


def _na_base(g, rows):
    return jnp.clip(NA_QROWS * g - NA_ROWS // 2, 0, rows - NA_KROWS)


def _na_patterns(rows):
    sigs, reps, pat_of_g = {}, [], []
    for g in range(rows // NA_QROWS):
        base = int(np.clip(NA_QROWS * g - NA_ROWS // 2, 0, rows - NA_KROWS))
        sig = (base - NA_QROWS * g,) + tuple(
            int(np.clip(r - NA_ROWS // 2, 0, rows - NA_ROWS)) - r for r in range(NA_QROWS * g, NA_QROWS * (g + 1)))
        if sig not in sigs:
            sigs[sig] = len(reps)
            reps.append(g)
        pat_of_g.append(sigs[sig])
    return reps, pat_of_g


def _na_bias(rpb, rows, reps):
    n_rr, n_cr = 2 * NA_ROWS - 1, 2 * NA_COLS - 1
    sel_r = np.zeros((len(reps), NA_QROWS, NA_KROWS, n_rr), np.float32)
    for p, g in enumerate(reps):
        base = int(np.clip(NA_QROWS * g - NA_ROWS // 2, 0, rows - NA_KROWS))
        for rho in range(NA_QROWS):
            r = NA_QROWS * g + rho
            r0 = int(np.clip(r - NA_ROWS // 2, 0, rows - NA_ROWS))
            for j in range(NA_KROWS):
                if r0 <= base + j < r0 + NA_ROWS:
                    sel_r[p, rho, j, base + j - r + NA_ROWS - 1] = 1.0
    sel_c = np.zeros((GRID_W, GRID_W, n_cr), np.float32)
    for c in range(GRID_W):
        c0 = int(np.clip(c - NA_COLS // 2, 0, GRID_W - NA_COLS))
        for kc in range(c0, c0 + NA_COLS):
            sel_c[c, kc, kc - c + NA_COLS - 1] = 1.0
    ok = (sel_r.sum(-1) > 0)[:, :, None, :, None] & (sel_c.sum(-1) > 0)[None, None, :, None, :]
    hp = lax.Precision.HIGHEST
    t = jnp.einsum('prjR,hRC->hprjC', jnp.asarray(sel_r), rpb.astype(F32), precision=hp)
    vals = jnp.einsum('hprjC,ckC->phrcjk', t, jnp.asarray(sel_c), precision=hp)
    bias = jnp.where(jnp.asarray(ok)[:, None], vals, NEG_BIG)
    return bias.reshape(len(reps), rpb.shape[0], NA_QROWS * GRID_W, NA_KROWS * GRID_W)


def _na_kernel(pat_ref, q_ref, k_ref, v_ref, bias_ref, o_ref, *, m, rows):
    del pat_ref
    j = pl.program_id(1)
    tq = q_ref.shape[0]
    qb = q_ref[...] * jnp.asarray(C_DH ** -0.5, BF16)
    lane = lax.broadcasted_iota(jnp.int32, qb.shape, 1)
    zero = jnp.zeros_like(qb)
    qs = jnp.concatenate([jnp.where(lane < C_DH, qb, zero), jnp.where(lane >= C_DH, qb, zero)], axis=0)
    k_cx = k_ref[0:m, :]
    v_cx = v_ref[0:m, :]
    nt = (((1,), (1,)), ((), ()))

    def emit(o):
        o_ref[...] = jnp.where(lane < C_DH, o[:tq], o[tq:]).astype(o_ref.dtype)

    def softmax_pv(parts):
        mx = None
        for s, _ in parts:
            pm = jnp.max(s, axis=-1, keepdims=True)
            mx = pm if mx is None else jnp.maximum(mx, pm)
        l = None
        o = None
        for s, v in parts:
            p = jnp.exp(s - mx)
            ps = jnp.sum(p, axis=-1, keepdims=True)
            po = jnp.dot(p.astype(BF16), v, preferred_element_type=F32)
            l = ps if l is None else l + ps
            o = po if o is None else o + po
        return o / l

    @pl.when(j < m // tq)
    def _():
        emit(softmax_pv([(lax.dot_general(qs, k_cx, nt, preferred_element_type=F32), v_cx)]))

    @pl.when(j >= m // tq)
    def _():
        nk = NA_KROWS * GRID_W
        off = pl.multiple_of(m + _na_base(j - m // tq, rows) * GRID_W, GRID_W)
        k_nb = k_ref[pl.ds(off, nk), :]
        v_nb = v_ref[pl.ds(off, nk), :]
        s_nb = lax.dot_general(qs, k_nb, nt, preferred_element_type=F32) + bias_ref[...].reshape(2 * tq, nk)
        s_cx = lax.dot_general(qs, k_cx, nt, preferred_element_type=F32)
        emit(softmax_pv([(s_nb, v_nb), (s_cx, v_cx)]))


def _na(z, rpb, m):
    T = z.shape[0]
    rows = (T - m) // GRID_W
    tq = NA_QROWS * GRID_W
    mq = m // tq
    reps, pat_of_g = _na_patterns(rows)
    bias = _na_bias(rpb, rows, reps)
    pat = jnp.asarray([0] * mq + pat_of_g, jnp.int32)
    grid_spec = pltpu.PrefetchScalarGridSpec(
        num_scalar_prefetch=1,
        grid=(C_HEADS // 2, T // tq),
        in_specs=[
            pl.BlockSpec((tq, LANES), lambda h, j, pat: (j, COL_CQ + h)),
            pl.BlockSpec((T, LANES), lambda h, j, pat: (0, COL_CK + h)),
            pl.BlockSpec((T, LANES), lambda h, j, pat: (0, COL_CV + h)),
            pl.BlockSpec((None, 2, tq, NA_KROWS * GRID_W), lambda h, j, pat: (pat[j], h, 0, 0)),
        ],
        out_specs=pl.BlockSpec((tq, LANES), lambda h, j, pat: (j, h)),
    )
    return pl.pallas_call(
        functools.partial(_na_kernel, m=m, rows=rows),
        grid_spec=grid_spec,
        out_shape=jax.ShapeDtypeStruct((T, C_HEADS * C_DH), BF16),
        compiler_params=_params(("parallel", "arbitrary")),
        name="nbr_attn",
    )(pat, z, z, z, bias)


def _merge_kernel(oa_ref, ob_ref, oc_ref, wb_ref, g0_ref, g1_ref, g2_ref, o_ref):
    acc = None
    for i, (o, g) in enumerate(((oa_ref, g0_ref), (ob_ref, g1_ref), (oc_ref, g2_ref))):
        t = jax.nn.sigmoid(g[...].astype(F32)) * jnp.dot(o[...], wb_ref[i], preferred_element_type=F32)
        acc = t if acc is None else acc + t
    o_ref[...] = acc.astype(o_ref.dtype)


def _merge(oa, ob, oc, wb, z, D, tm, tn):
    T = oa.shape[0]
    gcol = COL_G * LANES // tn
    nd = D // tn
    o_spec = pl.BlockSpec((tm, BRANCH_W), lambda i, j: (i, 0))
    return pl.pallas_call(
        _merge_kernel,
        grid=(T // tm, D // tn),
        in_specs=[o_spec, o_spec, o_spec,
                  pl.BlockSpec((N_BRANCH, BRANCH_W, tn), lambda i, j: (0, 0, j)),
                  pl.BlockSpec((tm, tn), lambda i, j: (i, gcol + j)),
                  pl.BlockSpec((tm, tn), lambda i, j: (i, gcol + nd + j)),
                  pl.BlockSpec((tm, tn), lambda i, j: (i, gcol + 2 * nd + j))],
        out_specs=pl.BlockSpec((tm, tn), lambda i, j: (i, j)),
        out_shape=jax.ShapeDtypeStruct((T, D), BF16),
        compiler_params=_params(("parallel", "arbitrary")),
        name="merge",
    )(oa, ob, oc, wb, z, z, z)


def _outproj_kernel(y_ref, w_ref, x_ref, gate_ref, o_ref, *, m):
    tm = y_ref.shape[0]
    r = jnp.dot(y_ref[...], w_ref[...], preferred_element_type=F32)
    row = pl.program_id(0) * tm + lax.broadcasted_iota(jnp.int32, r.shape, 0)
    g = jnp.where(row < m, gate_ref[1:2, :], gate_ref[0:1, :])
    o_ref[...] = x_ref[...] + g * r


def _outproj(y, w, xs, mod, gate_idx, m, tm, tn):
    T, D = xs.shape
    nd = D // tn
    return pl.pallas_call(
        functools.partial(_outproj_kernel, m=m),
        grid=(T // tm, nd),
        in_specs=[pl.BlockSpec((tm, D), lambda i, j: (i, 0)),
                  pl.BlockSpec((D, tn), lambda i, j: (0, j)),
                  pl.BlockSpec((tm, tn), lambda i, j: (i, j)),
                  pl.BlockSpec((SUBLANES, tn), lambda i, j: (0, gate_idx * nd + j))],
        out_specs=pl.BlockSpec((tm, tn), lambda i, j: (i, j)),
        out_shape=jax.ShapeDtypeStruct((T, D), F32),
        compiler_params=_params(("parallel", "arbitrary")),
        name="out_proj",
    )(y, w, xs, mod)


HI_MASK = 0xFFFF0000


def _pack_rows(v):
    half = v.shape[1] // 2
    u = lax.bitcast_convert_type(v.astype(BF16).astype(F32), jnp.uint32)
    return (u[:, half:] & jnp.uint32(HI_MASK)) | (u[:, :half] >> 16)


def _unpack_rows(w):
    lo = lax.bitcast_convert_type(w << 16, F32)
    hi = lax.bitcast_convert_type(w & jnp.uint32(HI_MASK), F32)
    return lo, hi

def _router_kernel(x_ref, g_ref, sh_ref, sc_ref, wr_ref, br_ref, h_ref, idx_ref, gate_ref, rank_ref, cnt_ref,
                   run_ref, *, m_tiles, n_experts):
    i = pl.program_id(0)
    is_ctx = i < m_tiles

    @pl.when(i == 0)
    def _():
        run_ref[...] = jnp.zeros_like(run_ref)

    x = x_ref[...]
    tr = x.shape[0]
    y = x * lax.rsqrt(jnp.mean(x * x, axis=-1, keepdims=True) + NORM_EPS) * g_ref[...]
    sh = jnp.where(is_ctx, sh_ref[1:2, :], sh_ref[0:1, :])
    sc = jnp.where(is_ctx, sc_ref[1:2, :], sc_ref[0:1, :])
    h = y * (1.0 + sc) + sh
    hb = h.astype(BF16)
    h_ref[...] = _pack_rows(h)

    h_lo = (h - hb.astype(F32)).astype(BF16)
    w = wr_ref[...]
    w_hi = w.astype(BF16)
    w_lo = (w - w_hi.astype(F32)).astype(BF16)
    logits = (jnp.dot(hb, w_hi, preferred_element_type=F32) + jnp.dot(h_lo, w_hi, preferred_element_type=F32)
              + jnp.dot(hb, w_lo, preferred_element_type=F32)) + br_ref[...]
    lane = lax.broadcasted_iota(jnp.int32, logits.shape, 1)
    work = jnp.where(lane < n_experts, logits, NEG_BIG)

    vals, idxs = [], []
    for _ in range(TOP_K):
        mx = jnp.max(work, axis=-1, keepdims=True)
        ix = jnp.min(jnp.where(work == mx, lane, LANES), axis=-1, keepdims=True)
        vals.append(mx)
        idxs.append(ix)
        work = jnp.where(lane == ix, NEG_BIG, work)
    es = [jnp.exp(v - vals[0]) for v in vals]
    den = es[0] + es[1] + es[2] + es[3]

    onehot = jnp.zeros(logits.shape, F32)
    for ix in idxs:
        onehot = onehot + jnp.where(lane == ix, 1.0, 0.0)
    r_i = lax.broadcasted_iota(jnp.int32, (tr, tr), 0)
    c_i = lax.broadcasted_iota(jnp.int32, (tr, tr), 1)
    tri = jnp.where(c_i < r_i, 1.0, 0.0).astype(BF16)
    before = jnp.dot(tri, onehot.astype(BF16), preferred_element_type=F32) + run_ref[...]

    idx_o = jnp.zeros(logits.shape, jnp.int32)
    gate_o = jnp.zeros(logits.shape, F32)
    rank_o = jnp.zeros(logits.shape, jnp.int32)
    for k in range(TOP_K):
        rk = jnp.sum(jnp.where(lane == idxs[k], before, 0.0), axis=-1, keepdims=True)
        idx_o = jnp.where(lane == k, idxs[k], idx_o)
        gate_o = jnp.where(lane == k, es[k] / den, gate_o)
        rank_o = jnp.where(lane == k, rk.astype(jnp.int32), rank_o)
    idx_ref[...] = idx_o
    gate_ref[...] = gate_o
    rank_ref[...] = rank_o
    run_ref[...] += jnp.sum(onehot, axis=0, keepdims=True)
    cnt_ref[...] = jnp.broadcast_to(run_ref[...], cnt_ref.shape)


def _router(xs, g, mod, shift_idx, scale_idx, w_router, b_router, m, tr):
    T, D = xs.shape
    E = w_router.shape[1]
    wr = jnp.pad(w_router, ((0, 0), (0, LANES - E)))
    br = jnp.pad(b_router, (0, LANES - E)).reshape(1, LANES)
    slab = pl.BlockSpec((tr, LANES), lambda i: (i, 0))
    return pl.pallas_call(
        functools.partial(_router_kernel, m_tiles=m // tr, n_experts=E),
        grid=(T // tr,),
        in_specs=[
            pl.BlockSpec((tr, D), lambda i: (i, 0)),
            pl.BlockSpec((1, D), lambda i: (0, 0)),
            pl.BlockSpec((SUBLANES, D), lambda i: (0, shift_idx)),
            pl.BlockSpec((SUBLANES, D), lambda i: (0, scale_idx)),
            pl.BlockSpec((D, LANES), lambda i: (0, 0)),
            pl.BlockSpec((1, LANES), lambda i: (0, 0)),
        ],
        out_specs=[pl.BlockSpec((tr, D // 2), lambda i: (i, 0)), slab, slab, slab,
                   pl.BlockSpec((SUBLANES, LANES), lambda i: (0, 0))],
        out_shape=[jax.ShapeDtypeStruct((T, D // 2), jnp.uint32), jax.ShapeDtypeStruct((T, LANES), jnp.int32),
                   jax.ShapeDtypeStruct((T, LANES), F32), jax.ShapeDtypeStruct((T, LANES), jnp.int32),
                   jax.ShapeDtypeStruct((SUBLANES, LANES), F32)],
        scratch_shapes=[pltpu.VMEM((1, LANES), F32)],
        compiler_params=_params(("arbitrary",)),
        name="router",
    )(xs, g.reshape(1, D), mod, mod, wr, br)


def _dispatch_kernel(dest_ref, h_ref, xs_in_ref, xs_ref, sem):
    del xs_in_ref
    tr = h_ref.shape[0]

    def copy(r, k):
        return pltpu.make_async_copy(h_ref.at[pl.ds(r, 1), :], xs_ref.at[pl.ds(dest_ref[0, r * TOP_K + k], 1), :], sem)

    def issue(r, c):
        for k in range(TOP_K):
            copy(r, k).start()
        return c

    lax.fori_loop(0, tr, issue, 0)

    def drain(r, c):
        for k in range(TOP_K):
            copy(r, k).wait()
        return c

    lax.fori_loop(0, tr, drain, 0)


def _dispatch(h, dest, n_rows, tr):
    T, D = h.shape
    xs0 = jnp.zeros((n_rows, D), h.dtype)
    return pl.pallas_call(
        _dispatch_kernel,
        grid=(T // tr,),
        in_specs=[
            pl.BlockSpec((None, 1, tr * TOP_K), lambda i: (i, 0, 0), memory_space=pltpu.SMEM),
            pl.BlockSpec((tr, D), lambda i: (i, 0)),
            pl.BlockSpec(memory_space=pl.ANY),
        ],
        out_specs=pl.BlockSpec(memory_space=pl.ANY),
        out_shape=jax.ShapeDtypeStruct((n_rows, D), h.dtype),
        scratch_shapes=[pltpu.SemaphoreType.DMA(())],
        input_output_aliases={2: 0},
        compiler_params=_params(("arbitrary",)),
        name="dispatch",
    )(dest.reshape(T // tr, 1, tr * TOP_K), h, xs0)


def _deinterleave(gu):
    r, width = gu.shape
    lane = lax.broadcasted_iota(jnp.int32, (r, LANES), 1)
    idx_even = (2 * lane) % LANES
    idx_odd = idx_even + 1
    first = lane < LANES // 2
    even, odd = [], []
    for c in range(width // (2 * LANES)):
        xa = gu[:, 2 * LANES * c:2 * LANES * c + LANES]
        xb = gu[:, 2 * LANES * c + LANES:2 * LANES * (c + 1)]
        even.append(jnp.where(first, jnp.take_along_axis(xa, idx_even, axis=1), jnp.take_along_axis(xb, idx_even, axis=1)))
        odd.append(jnp.where(first, jnp.take_along_axis(xa, idx_odd, axis=1), jnp.take_along_axis(xb, idx_odd, axis=1)))
    return jnp.concatenate(even, axis=1), jnp.concatenate(odd, axis=1)


def _expert_kernel(be_ref, nu_ref, x_ref, wgu_ref, bgu_ref, wd_ref, bd_ref, o_ref):
    i = pl.program_id(0)

    @pl.when(i < nu_ref[0])
    def _():
        x_lo, x_hi = _unpack_rows(x_ref[...])
        x = jnp.concatenate([x_lo.astype(BF16), x_hi.astype(BF16)], axis=1)
        gu = jnp.dot(x, wgu_ref[...], preferred_element_type=F32) + bgu_ref[...]
        glu, lin = _deinterleave(gu)
        glu = jnp.minimum(glu, SWIGLU_LIMIT)
        lin = jnp.clip(lin, -SWIGLU_LIMIT, SWIGLU_LIMIT)
        act = glu * jax.nn.sigmoid(SWIGLU_ALPHA * glu) * (lin + 1.0)
        y = jnp.dot(act.astype(BF16), wd_ref[...], preferred_element_type=F32) + bd_ref[...]
        o_ref[...] = _pack_rows(y)

    @pl.when(i >= nu_ref[0])
    def _():
        o_ref[...] = jnp.zeros_like(o_ref)


def _experts(xs, block_e, n_used, wgu, bgu, wd, bd, bm):
    n_rows, Dw = xs.shape
    E, D, F2 = wgu.shape
    F = F2 // 2
    grid_spec = pltpu.PrefetchScalarGridSpec(
        num_scalar_prefetch=2,
        grid=(n_rows // bm,),
        in_specs=[
            pl.BlockSpec((bm, Dw), lambda i, be, nu: (i, 0)),
            pl.BlockSpec((None, D, F2), lambda i, be, nu: (be[i], 0, 0)),
            pl.BlockSpec((None, 1, F2), lambda i, be, nu: (be[i], 0, 0)),
            pl.BlockSpec((None, F, D), lambda i, be, nu: (be[i], 0, 0)),
            pl.BlockSpec((None, 1, D), lambda i, be, nu: (be[i], 0, 0)),
        ],
        out_specs=pl.BlockSpec((bm, Dw), lambda i, be, nu: (i, 0)),
    )
    return pl.pallas_call(
        _expert_kernel,
        grid_spec=grid_spec,
        out_shape=jax.ShapeDtypeStruct((n_rows, Dw), jnp.uint32),
        compiler_params=_params(("arbitrary",)),
        name="experts",
    )(block_e, n_used, xs, wgu, bgu.reshape(E, 1, F2), wd, bd.reshape(E, 1, D))


def _combine_kernel(dest_ref, ys_ref, gatew_ref, x_ref, mod_ref, o_ref, buf, sem, *, m_tiles):
    tr = x_ref.shape[0]
    is_ctx = pl.program_id(0) < m_tiles

    def copy(r, k):
        return pltpu.make_async_copy(ys_ref.at[pl.ds(dest_ref[0, r * TOP_K + k], 1), :], buf.at[k, pl.ds(r, 1), :], sem)

    def issue(r, c):
        for k in range(TOP_K):
            copy(r, k).start()
        return c

    lax.fori_loop(0, tr, issue, 0)

    def drain(r, c):
        for k in range(TOP_K):
            copy(r, k).wait()
        return c

    lax.fori_loop(0, tr, drain, 0)

    half = x_ref.shape[1] // 2
    g = jnp.where(is_ctx, mod_ref[1:2, :], mod_ref[0:1, :])

    def piece(t, c):
        rs = pl.ds(pl.multiple_of(t * SUBLANES, SUBLANES), SUBLANES)
        gw = gatew_ref[rs, :]
        f_lo = f_hi = None
        for k in range(TOP_K):
            lo, hi = _unpack_rows(buf[k, rs, :])
            f_lo = gw[:, k:k + 1] * lo if f_lo is None else f_lo + gw[:, k:k + 1] * lo
            f_hi = gw[:, k:k + 1] * hi if f_hi is None else f_hi + gw[:, k:k + 1] * hi
        o_ref[rs, 0:half] = x_ref[rs, 0:half] + g[:, 0:half] * f_lo
        o_ref[rs, half:] = x_ref[rs, half:] + g[:, half:] * f_hi
        return c

    lax.fori_loop(0, tr // SUBLANES, piece, 0)


def _combine(ys, dest, gatew, xs, mod, gate_idx, m, tr):
    T, D = xs.shape
    return pl.pallas_call(
        functools.partial(_combine_kernel, m_tiles=m // tr),
        grid=(T // tr,),
        in_specs=[
            pl.BlockSpec((None, 1, tr * TOP_K), lambda i: (i, 0, 0), memory_space=pltpu.SMEM),
            pl.BlockSpec(memory_space=pl.ANY),
            pl.BlockSpec((tr, LANES), lambda i: (i, 0)),
            pl.BlockSpec((tr, D), lambda i: (i, 0)),
            pl.BlockSpec((SUBLANES, D), lambda i: (0, gate_idx)),
        ],
        out_specs=pl.BlockSpec((tr, D), lambda i: (i, 0)),
        out_shape=jax.ShapeDtypeStruct((T, D), F32),
        scratch_shapes=[pltpu.VMEM((TOP_K, tr, D // 2), jnp.uint32), pltpu.SemaphoreType.DMA(())],
        compiler_params=_params(("arbitrary",)),
        name="combine",
    )(dest.reshape(T // tr, 1, tr * TOP_K), ys, gatew, xs, mod)


def _final_norm_kernel(x_ref, g_ref, o_ref):
    x = x_ref[...]
    o_ref[...] = x * lax.rsqrt(jnp.mean(x * x, axis=-1, keepdims=True) + NORM_EPS) * g_ref[...]


def _final_norm(xs, g, m, n, tr):
    D = xs.shape[1]
    off = m // tr
    return pl.pallas_call(
        _final_norm_kernel,
        grid=(n // tr,),
        in_specs=[pl.BlockSpec((tr, D), lambda i: (i + off, 0)), pl.BlockSpec((1, D), lambda i: (0, 0))],
        out_specs=pl.BlockSpec((tr, D), lambda i: (i, 0)),
        out_shape=jax.ShapeDtypeStruct((n, D), F32),
        compiler_params=_params(("parallel",)),
        name="final_norm",
    )(xs, g.reshape(1, D))


def kernel(x, c, ctx, c_ctx, w_ada, b_ada, norm1_g, norm2_g, w_in, lam_qk, subln_g, conv_w, conv_b, lru_gate_w,
           lru_gate_b, lru_lambda, na_rpb, w_branch, w_out, w_router, b_router, w_gu, b_gu, w_down, b_down, final_g):
    B, n, D = x.shape
    m = ctx.shape[1]
    assert B == 1 and n % (NA_QROWS * GRID_W) == 0
    L = w_ada.shape[0]
    E = w_router.shape[-1]
    T = m + n
    rows = n // GRID_W
    tr = 256
    assert m % tr == 0 and T % tr == 0 and m % (NA_QROWS * GRID_W) == 0
    tm = _tile(T, 1408, 128)
    tn = _tile(D, 512, LANES)
    bm = 256
    n_blocks = -(-(T * TOP_K + E * (bm - 1)) // bm)
    n_rows = n_blocks * bm

    xs = jnp.concatenate([ctx[0], x[0]], axis=0)
    mod_all = _ada(c, c_ctx, w_ada, b_ada)
    cos, sin = _rope_tables(m, n)

    for l in range(L):
        lambda_init = 0.8 - 0.6 * math.exp(-0.3 * l)
        mod = mod_all[l]
        h = _norm_mod(xs, norm1_g[l], mod, 0, 1, m, tr)
        z = _matmul(h, w_in[l].astype(BF16), tm, tn, BF16)
        qk = _rope(z, cos, sin, tr)
        o_a = _attn_a(qk, z, lam_qk[l], subln_g[l], lambda_init, m, 256, 1536)
        o_b = _rglru(z, conv_w[l], conv_b[l], lru_gate_w[l], lru_gate_b[l], lru_lambda[l], m, 256)
        o_c = _na(z, na_rpb[l], m)
        y = _merge(o_a, o_b, o_c, w_branch[l].astype(BF16), z, D, tm, tn)
        xs = _outproj(y, w_out[l].astype(BF16), xs, mod, 2, m, tm, tn)

        h2, idx, gatew, rank, cnt = _router(xs, norm2_g[l], mod, 3, 4, w_router[l], b_router[l], m, tr)
        counts = cnt[0, :E].astype(jnp.int32)
        padded = (counts + bm - 1) // bm * bm
        pend = jnp.cumsum(padded)
        pstart = pend - padded
        dest = (pstart[idx[:, :TOP_K]] + rank[:, :TOP_K]).astype(jnp.int32)
        block_start = jnp.arange(n_blocks, dtype=jnp.int32) * bm
        block_e = jnp.minimum(jnp.sum((pend[None, :] <= block_start[:, None]).astype(jnp.int32), axis=1), E - 1)
        n_used = (pend[-1:] // bm).astype(jnp.int32)

        xg = _dispatch(h2, dest, n_rows, tr)
        ys = _experts(xg, block_e, n_used, w_gu[l].astype(BF16), b_gu[l], w_down[l].astype(BF16), b_down[l], bm)
        xs = _combine(ys, dest, gatew, xs, mod, 5, m, tr)

    return _final_norm(xs, final_g, m, n, tr)[None]
```

```python
import functools
import math

import numpy as np
import jax
import jax.numpy as jnp
from jax import lax
from jax.experimental import pallas as pl
from jax.experimental.pallas import tpu as pltpu

F32 = jnp.float32
BF16 = jnp.bfloat16

GRID_W = 64
NORM_EPS = 1e-6
A_HEADS, A_DH = 8, 64
ROPE_PAIRS = A_DH // 4
ROPE_BASE = 10000.0
B_WIDTH, B_BW, B_CONV, CONV_LEFT, LRU_C = 1024, 128, 4, 2, 8.0
C_HEADS, C_DH, NA_ROWS, NA_COLS = 16, 64, 8, 16
N_BRANCH, BRANCH_W = 3, 1024
TOP_K = 4
SWIGLU_ALPHA, SWIGLU_LIMIT = 1.702, 7.0

LANES = 128
SUBLANES = 8
COL_AQ, COL_AK, COL_AV, COL_BX, COL_BY, COL_CQ, COL_CK, COL_CV, COL_G = 0, 8, 16, 24, 32, 40, 48, 56, 64
NEG_BIG = -1e30
VMEM_LIMIT = 56 * 1024 * 1024


def _params(sem):
    return pltpu.CompilerParams(dimension_semantics=sem, vmem_limit_bytes=VMEM_LIMIT)


def _tile(dim, target, mult):
    best = None
    for t in range(mult, min(dim, target) + 1, mult):
        if dim % t == 0:
            best = t
    assert best is not None, (dim, target, mult)
    return best


def _ada_kernel(c_ref, w_ref, b_ref, o_ref, acc_ref):
    k = pl.program_id(2)

    @pl.when(k == 0)
    def _():
        acc_ref[...] = jnp.zeros_like(acc_ref)

    cc = c_ref[...]
    s = cc * jax.nn.sigmoid(cc)
    hi = s.astype(BF16)
    lo = (s - hi.astype(F32)).astype(BF16)
    row = lax.broadcasted_iota(jnp.int32, s.shape, 0)
    lhs = jnp.where(row < 2, hi, lo)
    acc_ref[...] += jnp.dot(lhs, w_ref[...].astype(BF16), preferred_element_type=F32)

    @pl.when(k == pl.num_programs(2) - 1)
    def _():
        acc = acc_ref[...]
        o_ref[...] = acc + pltpu.roll(acc, SUBLANES - 2, 0) + b_ref[...]


def _ada(c, c_ctx, w_ada, b_ada):
    L, D, N = w_ada.shape
    cc = jnp.concatenate([c[:1], c_ctx[None], c[:1], c_ctx[None], jnp.zeros((4, D), F32)], axis=0)
    tk = _tile(D, 1024, LANES)
    tn = _tile(N, 2048, LANES)
    return pl.pallas_call(
        _ada_kernel,
        grid=(L, N // tn, D // tk),
        in_specs=[
            pl.BlockSpec((SUBLANES, tk), lambda l, j, k: (0, k)),
            pl.BlockSpec((None, tk, tn), lambda l, j, k: (l, k, j)),
            pl.BlockSpec((None, 1, tn), lambda l, j, k: (l, 0, j)),
        ],
        out_specs=pl.BlockSpec((None, SUBLANES, tn), lambda l, j, k: (l, 0, j)),
        out_shape=jax.ShapeDtypeStruct((L, SUBLANES, N), F32),
        scratch_shapes=[pltpu.VMEM((SUBLANES, tn), F32)],
        compiler_params=_params(("parallel", "parallel", "arbitrary")),
        name="ada",
    )(cc, w_ada, b_ada.reshape(L, 1, N))


def _norm_mod_kernel(x_ref, g_ref, sh_ref, sc_ref, o_ref, *, m_tiles):
    is_ctx = pl.program_id(0) < m_tiles
    x = x_ref[...]
    y = x * lax.rsqrt(jnp.mean(x * x, axis=-1, keepdims=True) + NORM_EPS) * g_ref[...]
    sh = jnp.where(is_ctx, sh_ref[1:2, :], sh_ref[0:1, :])
    sc = jnp.where(is_ctx, sc_ref[1:2, :], sc_ref[0:1, :])
    o_ref[...] = (y * (1.0 + sc) + sh).astype(o_ref.dtype)


def _norm_mod(xs, g, mod, shift_idx, scale_idx, m, tr):
    T, D = xs.shape
    return pl.pallas_call(
        functools.partial(_norm_mod_kernel, m_tiles=m // tr),
        grid=(T // tr,),
        in_specs=[
            pl.BlockSpec((tr, D), lambda i: (i, 0)),
            pl.BlockSpec((1, D), lambda i: (0, 0)),
            pl.BlockSpec((SUBLANES, D), lambda i: (0, shift_idx)),
            pl.BlockSpec((SUBLANES, D), lambda i: (0, scale_idx)),
        ],
        out_specs=pl.BlockSpec((tr, D), lambda i: (i, 0)),
        out_shape=jax.ShapeDtypeStruct((T, D), BF16),
        compiler_params=_params(("parallel",)),
        name="norm_mod",
    )(xs, g.reshape(1, D), mod, mod)


def _mm_kernel(a_ref, b_ref, o_ref):
    o_ref[...] = jnp.dot(a_ref[...], b_ref[...].astype(BF16), preferred_element_type=F32).astype(o_ref.dtype)


def _matmul(a, b_all, l, tm, tn, out_dtype):
    M, K = a.shape
    N = b_all.shape[-1]
    return pl.pallas_call(
        _mm_kernel,
        grid=(M // tm, N // tn),
        in_specs=[pl.BlockSpec((tm, K), lambda i, j: (i, 0)), pl.BlockSpec((None, K, tn), lambda i, j: (l, 0, j))],
        out_specs=pl.BlockSpec((tm, tn), lambda i, j: (i, j)),
        out_shape=jax.ShapeDtypeStruct((M, N), out_dtype),
        compiler_params=_params(("parallel", "arbitrary")),
        name="in_proj",
    )(a, b_all)


def _rope_tables(m, n):
    t = np.arange(n)
    row = (t // GRID_W).astype(np.float32)
    col = (t % GRID_W).astype(np.float32)
    inv = jnp.asarray(ROPE_BASE, F32) ** (-jnp.arange(ROPE_PAIRS, dtype=F32) / ROPE_PAIRS)
    ang_r = jnp.asarray(row)[:, None] * inv
    ang_c = jnp.asarray(col)[:, None] * inv
    lane = np.arange(LANES)
    d = lane % A_DH
    use_col = (d // (A_DH // 2)) == 1
    within = d % (A_DH // 2)
    pair = within % ROPE_PAIRS
    sign = np.where(within < ROPE_PAIRS, -1.0, 1.0).astype(np.float32)
    ang = jnp.where(jnp.asarray(use_col)[None, :], ang_c[:, pair], ang_r[:, pair])
    cos = jnp.concatenate([jnp.ones((m, LANES), F32), jnp.cos(ang)], axis=0)
    sin = jnp.concatenate([jnp.zeros((m, LANES), F32), jnp.sin(ang) * jnp.asarray(sign)[None, :]], axis=0)
    return cos, sin


def _rope_kernel(z_ref, cos_ref, sin_ref, o_ref):
    scale = jnp.where(pl.program_id(1) == 0, A_DH ** -0.5, 1.0).astype(F32)
    cos = cos_ref[...]
    sin = sin_ref[...]
    lane = lax.broadcasted_iota(jnp.int32, cos.shape, 1)
    lo = (lane % (2 * ROPE_PAIRS)) < ROPE_PAIRS
    for hc in range(z_ref.shape[1] // LANES):
        x = z_ref[:, hc * LANES:(hc + 1) * LANES].astype(F32)
        partner = jnp.where(lo, pltpu.roll(x, LANES - ROPE_PAIRS, 1), pltpu.roll(x, ROPE_PAIRS, 1))
        o_ref[:, hc * LANES:(hc + 1) * LANES] = ((x * cos + partner * sin) * scale).astype(o_ref.dtype)


def _rope(z, cos, sin, tr):
    T = z.shape[0]
    W = A_HEADS * 2 * A_DH
    return pl.pallas_call(
        _rope_kernel,
        grid=(T // tr, 2),
        in_specs=[
            pl.BlockSpec((tr, W), lambda i, j: (i, j)),
            pl.BlockSpec((tr, LANES), lambda i, j: (i, 0)),
            pl.BlockSpec((tr, LANES), lambda i, j: (i, 0)),
        ],
        out_specs=pl.BlockSpec((tr, W), lambda i, j: (i, j)),
        out_shape=jax.ShapeDtypeStruct((T, 2 * W), BF16),
        compiler_params=_params(("parallel", "arbitrary")),
        name="rope",
    )(z, cos, sin)


def _attn_a_kernel(lam_ref, g_ref, q_ref, k_ref, v_ref, o_ref, *, m, T, tk, lambda_init):
    qi = pl.program_id(1)
    qb = q_ref[...]
    tq = qb.shape[0]
    lane = lax.broadcasted_iota(jnp.int32, qb.shape, 1)
    zero = jnp.zeros_like(qb)
    qs = jnp.concatenate([jnp.where(lane < A_DH, qb, zero), jnp.where(lane >= A_DH, qb, zero)], axis=0)

    def scores(kc):
        return lax.dot_general(qs, kc, (((1,), (1,)), ((), ())), preferred_element_type=F32)

    def step(s, vc, carry):
        m_i, l_i, acc = carry
        m_new = jnp.maximum(m_i, jnp.max(s, axis=-1, keepdims=True))
        alpha = jnp.exp(m_i - m_new)
        p = jnp.exp(s - m_new)
        l_new = alpha * l_i + jnp.sum(p, axis=-1, keepdims=True)
        acc = alpha * acc + jnp.dot(p.astype(BF16), vc, preferred_element_type=F32)
        return m_new, l_new, acc

    init = (jnp.full((2 * tq, 1), NEG_BIG, F32), jnp.zeros((2 * tq, 1), F32), jnp.zeros((2 * tq, LANES), F32))

    lq = lam_ref[...]
    lam = (jnp.exp(jnp.sum(lq[0:1] * lq[1:2], axis=-1, keepdims=True))
           - jnp.exp(jnp.sum(lq[2:3] * lq[3:4], axis=-1, keepdims=True)) + lambda_init)

    def finish(carry):
        _, l_i, acc = carry
        o = acc / l_i
        d = o[:tq] - lam * o[tq:]
        y = d * lax.rsqrt(jnp.mean(d * d, axis=-1, keepdims=True) + NORM_EPS) * g_ref[...]
        o_ref[...] = (y * (1.0 - lambda_init)).astype(o_ref.dtype)

    @pl.when(qi < m // tq)
    def _():
        finish(step(scores(k_ref[0:m, :]), v_ref[0:m, :], init))

    @pl.when(qi >= m // tq)
    def _():
        bounds = [(lo, min(lo + tk, T)) for lo in range(0, T, tk)]
        carry = init
        s = scores(k_ref[bounds[0][0]:bounds[0][1], :])
        for c, (lo, hi) in enumerate(bounds):
            s_next = scores(k_ref[bounds[c + 1][0]:bounds[c + 1][1], :]) if c + 1 < len(bounds) else None
            carry = step(s, v_ref[lo:hi, :], carry)
            s = s_next
        finish(carry)


def _attn_a(qk, z, lam_qk, subln_g, lambda_init, m, tq, tk):
    T = z.shape[0]
    return pl.pallas_call(
        functools.partial(_attn_a_kernel, m=m, T=T, tk=tk, lambda_init=lambda_init),
        grid=(A_HEADS, T // tq),
        in_specs=[
            pl.BlockSpec((4, A_DH), lambda h, i: (0, 0)),
            pl.BlockSpec((1, 2 * A_DH), lambda h, i: (0, 0)),
            pl.BlockSpec((tq, LANES), lambda h, i: (i, h)),
            pl.BlockSpec((T, LANES), lambda h, i: (0, A_HEADS + h)),
            pl.BlockSpec((T, LANES), lambda h, i: (0, COL_AV + h)),
        ],
        out_specs=pl.BlockSpec((tq, LANES), lambda h, i: (i, h)),
        out_shape=jax.ShapeDtypeStruct((T, A_HEADS * 2 * A_DH), BF16),
        compiler_params=_params(("parallel", "arbitrary")),
        name="diff_attn",
    )(lam_qk, subln_g.reshape(1, 2 * A_DH), qk, qk, z)


def _rglru_kernel(bx_ref, by_ref, cw_ref, cb_ref, gw_ref, gb_ref, lam_ref, o_ref, xpad, hf, *, m, T, tc):
    nchunk = T // tc
    mc = m // tc
    nb = tc // SUBLANES
    halo = SUBLANES
    xpad[0:halo, :] = jnp.zeros((halo, LANES), F32)
    xpad[halo + T:2 * halo + T, :] = jnp.zeros((halo, LANES), F32)
    xpad[halo:halo + T, :] = bx_ref[...].astype(F32)

    row = lax.broadcasted_iota(jnp.int32, (tc, LANES), 0)
    sub = lax.broadcasted_iota(jnp.int32, (nb, SUBLANES, LANES), 1)
    cw = cw_ref[...]
    cb = cb_ref[...]

    def coeffs(j, d):
        r0 = j * tc
        left_cut = r0 == m
        right_cut = (r0 + tc) == m
        x_m2 = xpad[pl.ds(r0 + halo - 2, tc), :]
        x_m1 = xpad[pl.ds(r0 + halo - 1, tc), :]
        x_0 = xpad[pl.ds(r0 + halo, tc), :]
        x_p1 = xpad[pl.ds(r0 + halo + 1, tc), :]
        x_m2 = jnp.where(jnp.logical_and(left_cut, row < 2), 0.0, x_m2)
        x_m1 = jnp.where(jnp.logical_and(left_cut, row < 1), 0.0, x_m1)
        x_p1 = jnp.where(jnp.logical_and(right_cut, row >= tc - 1), 0.0, x_p1)
        u = cb + x_m2 * cw[0:1] + x_m1 * cw[1:2] + x_0 * cw[2:3] + x_p1 * cw[3:4]
        ub = u.astype(BF16)
        gr = jnp.dot(ub, gw_ref[d, 0].astype(BF16), preferred_element_type=F32) + gb_ref[d, 0]
        gi = jnp.dot(ub, gw_ref[d, 1].astype(BF16), preferred_element_type=F32) + gb_ref[d, 1]
        r = jax.nn.sigmoid(gr)
        i = jax.nn.sigmoid(gi)
        log_a = (-LRU_C * jax.nn.softplus(-lam_ref[d])) * r
        a = jnp.exp(log_a)
        b = jnp.sqrt(-jnp.tanh(log_a) * (a * a + 1.0)) * (i * u)
        return a, b

    def chunk_scan(a, b, h, rev):
        A = a.reshape(nb, SUBLANES, LANES)
        B = b.reshape(nb, SUBLANES, LANES)
        for s in (1, 2, 4):
            shift = SUBLANES - s if rev else s
            msk = (sub < SUBLANES - s) if rev else (sub >= s)
            Ap = pltpu.roll(A, shift, 1)
            Bp = pltpu.roll(B, shift, 1)
            B = jnp.where(msk, A * Bp + B, B)
            A = jnp.where(msk, A * Ap, A)
        outs = [None] * nb
        order = range(nb - 1, -1, -1) if rev else range(nb)
        for jb in order:
            hb = A[jb] * h + B[jb]
            outs[jb] = hb
            edge = hb[0:1, :] if rev else hb[SUBLANES - 1:SUBLANES, :]
            h = jnp.broadcast_to(edge, (SUBLANES, LANES))
        return jnp.concatenate(outs, axis=0), h

    def fwd_body(j, h):
        a, b = coeffs(j, 0)
        hc, h = chunk_scan(a, b, h, False)
        hf[pl.ds(pl.multiple_of(j * tc, tc), tc), :] = hc
        return h

    lax.fori_loop(0, nchunk, fwd_body, jnp.zeros((SUBLANES, LANES), F32))

    def bwd_body(s, h):
        j = jnp.where(s < mc, mc - 1 - s, nchunk - 1 - (s - mc))
        a, b = coeffs(j, 1)
        hc, h = chunk_scan(a, b, h, True)
        r0 = pl.multiple_of(j * tc, tc)
        by = by_ref[pl.ds(r0, tc), :].astype(F32)
        o_ref[pl.ds(r0, tc), :] = ((hf[pl.ds(r0, tc), :] + hc) * jax.nn.gelu(by)).astype(o_ref.dtype)
        return h

    lax.fori_loop(0, nchunk, bwd_body, jnp.zeros((SUBLANES, LANES), F32))


def _rglru(z, conv_w, conv_b, gate_w, gate_b, lru_lambda, m, tc):
    T = z.shape[0]
    nblk = B_WIDTH // B_BW
    return pl.pallas_call(
        functools.partial(_rglru_kernel, m=m, T=T, tc=tc),
        grid=(nblk,),
        in_specs=[
            pl.BlockSpec((T, LANES), lambda c: (0, COL_BX + c)),
            pl.BlockSpec((T, LANES), lambda c: (0, COL_BY + c)),
            pl.BlockSpec((B_CONV, LANES), lambda c: (0, c)),
            pl.BlockSpec((1, LANES), lambda c: (0, c)),
            pl.BlockSpec((2, 2, None, B_BW, B_BW), lambda c: (0, 0, c, 0, 0)),
            pl.BlockSpec((2, 2, None, 1, B_BW), lambda c: (0, 0, c, 0, 0)),
            pl.BlockSpec((2, 1, LANES), lambda c: (0, 0, c)),
        ],
        out_specs=pl.BlockSpec((T, LANES), lambda c: (0, c)),
        out_shape=jax.ShapeDtypeStruct((T, B_WIDTH), BF16),
        scratch_shapes=[pltpu.VMEM((T + 2 * SUBLANES, LANES), F32), pltpu.VMEM((T, LANES), F32)],
        compiler_params=_params(("parallel",)),
        name="rglru",
    )(z, z, conv_w, conv_b.reshape(1, B_WIDTH), gate_w, gate_b.reshape(2, 2, nblk, 1, B_BW),
      lru_lambda.reshape(2, 1, B_WIDTH))


NA_QROWS = 4
NA_KROWS = NA_QROWS + NA_ROWS - 1


def _na_base(g, rows):
    return jnp.clip(NA_QROWS * g - NA_ROWS // 2, 0, rows - NA_KROWS)


def _na_patterns(rows):
    sigs, reps, pat_of_g = {}, [], []
    for g in range(rows // NA_QROWS):
        base = int(np.clip(NA_QROWS * g - NA_ROWS // 2, 0, rows - NA_KROWS))
        sig = (base - NA_QROWS * g,) + tuple(
            int(np.clip(r - NA_ROWS // 2, 0, rows - NA_ROWS)) - r for r in range(NA_QROWS * g, NA_QROWS * (g + 1)))
        if sig not in sigs:
            sigs[sig] = len(reps)
            reps.append(g)
        pat_of_g.append(sigs[sig])
    return reps, pat_of_g


def _na_bias(rpb, rows, reps):
    n_rr, n_cr = 2 * NA_ROWS - 1, 2 * NA_COLS - 1
    sel_r = np.zeros((len(reps), NA_QROWS, NA_KROWS, n_rr), np.float32)
    for p, g in enumerate(reps):
        base = int(np.clip(NA_QROWS * g - NA_ROWS // 2, 0, rows - NA_KROWS))
        for rho in range(NA_QROWS):
            r = NA_QROWS * g + rho
            r0 = int(np.clip(r - NA_ROWS // 2, 0, rows - NA_ROWS))
            for j in range(NA_KROWS):
                if r0 <= base + j < r0 + NA_ROWS:
                    sel_r[p, rho, j, base + j - r + NA_ROWS - 1] = 1.0
    sel_c = np.zeros((GRID_W, GRID_W, n_cr), np.float32)
    for c in range(GRID_W):
        c0 = int(np.clip(c - NA_COLS // 2, 0, GRID_W - NA_COLS))
        for kc in range(c0, c0 + NA_COLS):
            sel_c[c, kc, kc - c + NA_COLS - 1] = 1.0
    ok = (sel_r.sum(-1) > 0)[:, :, None, :, None] & (sel_c.sum(-1) > 0)[None, None, :, None, :]
    hp = lax.Precision.HIGHEST
    t = jnp.einsum('prjR,hRC->hprjC', jnp.asarray(sel_r), rpb.astype(F32), precision=hp)
    vals = jnp.einsum('hprjC,ckC->phrcjk', t, jnp.asarray(sel_c), precision=hp)
    bias = jnp.where(jnp.asarray(ok)[:, None], vals, NEG_BIG)
    return bias.reshape(len(reps), rpb.shape[0], NA_QROWS * GRID_W, NA_KROWS * GRID_W)


def _na_kernel(pat_ref, q_ref, k_ref, v_ref, bias_ref, o_ref, *, m, rows):
    tq = NA_QROWS * GRID_W
    nk = NA_KROWS * GRID_W
    lane = lax.broadcasted_iota(jnp.int32, (tq, LANES), 1)
    nt = (((1,), (1,)), ((), ()))

    def stacked_q(row0):
        qb = q_ref[pl.ds(row0, tq), :] * jnp.asarray(C_DH ** -0.5, BF16)
        zero = jnp.zeros_like(qb)
        return jnp.concatenate([jnp.where(lane < C_DH, qb, zero), jnp.where(lane >= C_DH, qb, zero)], axis=0)

    def emit(row0, o):
        o_ref[pl.ds(row0, tq), :] = jnp.where(lane < C_DH, o[:tq], o[tq:]).astype(o_ref.dtype)

    def softmax_pv(parts):
        mx = None
        for s, _ in parts:
            pm = jnp.max(s, axis=-1, keepdims=True)
            mx = pm if mx is None else jnp.maximum(mx, pm)
        l = None
        o = None
        for s, v in parts:
            p = jnp.exp(s - mx)
            ps = jnp.sum(p, axis=-1, keepdims=True)
            po = jnp.dot(p.astype(BF16), v, preferred_element_type=F32)
            l = ps if l is None else l + ps
            o = po if o is None else o + po
        return o / l

    for c in range(m // tq):
        emit(c * tq, softmax_pv([(lax.dot_general(stacked_q(c * tq), k_ref[0:m, :], nt, preferred_element_type=F32),
                                  v_ref[0:m, :])]))

    def group(g, carry):
        row0 = pl.multiple_of(m + g * tq, tq)
        off = pl.multiple_of(m + _na_base(g, rows) * GRID_W, GRID_W)
        qs = stacked_q(row0)
        bias = bias_ref[pat_ref[g]].reshape(2 * tq, nk)
        s_nb = lax.dot_general(qs, k_ref[pl.ds(off, nk), :], nt, preferred_element_type=F32) + bias
        s_cx = lax.dot_general(qs, k_ref[0:m, :], nt, preferred_element_type=F32)
        emit(row0, softmax_pv([(s_nb, v_ref[pl.ds(off, nk), :]), (s_cx, v_ref[0:m, :])]))
        return carry

    lax.fori_loop(0, rows // NA_QROWS, group, 0, unroll=2)


def _na(z, rpb, m):
    T = z.shape[0]
    rows = (T - m) // GRID_W
    tq = NA_QROWS * GRID_W
    assert m % tq == 0 and (rows // NA_QROWS) % 2 == 0
    reps, pat_of_g = _na_patterns(rows)
    bias = _na_bias(rpb, rows, reps)
    pat = jnp.asarray(pat_of_g, jnp.int32)
    col = lambda base: pl.BlockSpec((T, LANES), lambda h, pat: (0, base + h))
    grid_spec = pltpu.PrefetchScalarGridSpec(
        num_scalar_prefetch=1,
        grid=(C_HEADS // 2,),
        in_specs=[col(COL_CQ), col(COL_CK), col(COL_CV),
                  pl.BlockSpec((len(reps), 2, tq, NA_KROWS * GRID_W), lambda h, pat: (0, h, 0, 0))],
        out_specs=pl.BlockSpec((T, LANES), lambda h, pat: (0, h)),
    )
    return pl.pallas_call(
        functools.partial(_na_kernel, m=m, rows=rows),
        grid_spec=grid_spec,
        out_shape=jax.ShapeDtypeStruct((T, C_HEADS * C_DH), BF16),
        compiler_params=_params(("parallel",)),
        name="nbr_attn",
    )(pat, z, z, z, bias)


def _merge_kernel(oa_ref, ob_ref, oc_ref, wb_ref, g0_ref, g1_ref, g2_ref, o_ref):
    acc = None
    for i, (o, g) in enumerate(((oa_ref, g0_ref), (ob_ref, g1_ref), (oc_ref, g2_ref))):
        t = jax.nn.sigmoid(g[...].astype(F32)) * jnp.dot(o[...], wb_ref[i], preferred_element_type=F32)
        acc = t if acc is None else acc + t
    o_ref[...] = acc.astype(o_ref.dtype)


def _merge(oa, ob, oc, wb_all, l, z, D, tm, tn):
    T = oa.shape[0]
    gcol = COL_G * LANES // tn
    nd = D // tn
    o_spec = pl.BlockSpec((tm, BRANCH_W), lambda i, j: (i, 0))
    return pl.pallas_call(
        _merge_kernel,
        grid=(T // tm, D // tn),
        in_specs=[o_spec, o_spec, o_spec,
                  pl.BlockSpec((None, N_BRANCH, BRANCH_W, tn), lambda i, j: (l, 0, 0, j)),
                  pl.BlockSpec((tm, tn), lambda i, j: (i, gcol + j)),
                  pl.BlockSpec((tm, tn), lambda i, j: (i, gcol + nd + j)),
                  pl.BlockSpec((tm, tn), lambda i, j: (i, gcol + 2 * nd + j))],
        out_specs=pl.BlockSpec((tm, tn), lambda i, j: (i, j)),
        out_shape=jax.ShapeDtypeStruct((T, D), BF16),
        compiler_params=_params(("parallel", "arbitrary")),
        name="merge",
    )(oa, ob, oc, wb_all, z, z, z)


def _outproj_kernel(y_ref, w_ref, x_ref, gate_ref, o_ref, *, m):
    tm = y_ref.shape[0]
    r = jnp.dot(y_ref[...], w_ref[...], preferred_element_type=F32)
    row = pl.program_id(0) * tm + lax.broadcasted_iota(jnp.int32, r.shape, 0)
    g = jnp.where(row < m, gate_ref[1:2, :], gate_ref[0:1, :])
    o_ref[...] = x_ref[...] + g * r


def _outproj(y, w_all, l, xs, mod, gate_idx, m, tm, tn):
    T, D = xs.shape
    nd = D // tn
    return pl.pallas_call(
        functools.partial(_outproj_kernel, m=m),
        grid=(T // tm, nd),
        in_specs=[pl.BlockSpec((tm, D), lambda i, j: (i, 0)),
                  pl.BlockSpec((None, D, tn), lambda i, j: (l, 0, j)),
                  pl.BlockSpec((tm, tn), lambda i, j: (i, j)),
                  pl.BlockSpec((SUBLANES, tn), lambda i, j: (0, gate_idx * nd + j))],
        out_specs=pl.BlockSpec((tm, tn), lambda i, j: (i, j)),
        out_shape=jax.ShapeDtypeStruct((T, D), F32),
        compiler_params=_params(("parallel", "arbitrary")),
        name="out_proj",
    )(y, w_all, xs, mod)


HI_MASK = 0xFFFF0000


def _pack_rows(v):
    half = v.shape[1] // 2
    u = lax.bitcast_convert_type(v.astype(BF16).astype(F32), jnp.uint32)
    return (u[:, half:] & jnp.uint32(HI_MASK)) | (u[:, :half] >> 16)


def _unpack_rows(w):
    lo = lax.bitcast_convert_type(w << 16, F32)
    hi = lax.bitcast_convert_type(w & jnp.uint32(HI_MASK), F32)
    return lo, hi

def _router_kernel(x_ref, g_ref, sh_ref, sc_ref, wr_ref, br_ref, h_ref, idx_ref, gate_ref, rank_ref, cnt_ref,
                   run_ref, *, m_tiles, n_experts):
    i = pl.program_id(0)
    is_ctx = i < m_tiles

    @pl.when(i == 0)
    def _():
        run_ref[...] = jnp.zeros_like(run_ref)

    x = x_ref[...]
    tr = x.shape[0]
    y = x * lax.rsqrt(jnp.mean(x * x, axis=-1, keepdims=True) + NORM_EPS) * g_ref[...]
    sh = jnp.where(is_ctx, sh_ref[1:2, :], sh_ref[0:1, :])
    sc = jnp.where(is_ctx, sc_ref[1:2, :], sc_ref[0:1, :])
    h = y * (1.0 + sc) + sh
    hb = h.astype(BF16)
    h_ref[...] = _pack_rows(h)

    h_lo = (h - hb.astype(F32)).astype(BF16)
    w = wr_ref[...]
    w_hi = w.astype(BF16)
    w_lo = (w - w_hi.astype(F32)).astype(BF16)
    logits = (jnp.dot(hb, w_hi, preferred_element_type=F32) + jnp.dot(h_lo, w_hi, preferred_element_type=F32)
              + jnp.dot(hb, w_lo, preferred_element_type=F32)) + br_ref[...]
    lane = lax.broadcasted_iota(jnp.int32, logits.shape, 1)
    work = jnp.where(lane < n_experts, logits, NEG_BIG)

    vals, idxs = [], []
    for _ in range(TOP_K):
        mx = jnp.max(work, axis=-1, keepdims=True)
        ix = jnp.min(jnp.where(work == mx, lane, LANES), axis=-1, keepdims=True)
        vals.append(mx)
        idxs.append(ix)
        work = jnp.where(lane == ix, NEG_BIG, work)
    es = [jnp.exp(v - vals[0]) for v in vals]
    den = es[0] + es[1] + es[2] + es[3]

    onehot = jnp.zeros(logits.shape, F32)
    for ix in idxs:
        onehot = onehot + jnp.where(lane == ix, 1.0, 0.0)
    r_i = lax.broadcasted_iota(jnp.int32, (tr, tr), 0)
    c_i = lax.broadcasted_iota(jnp.int32, (tr, tr), 1)
    tri = jnp.where(c_i < r_i, 1.0, 0.0).astype(BF16)
    before = jnp.dot(tri, onehot.astype(BF16), preferred_element_type=F32) + run_ref[...]

    idx_o = jnp.zeros(logits.shape, jnp.int32)
    gate_o = jnp.zeros(logits.shape, F32)
    rank_o = jnp.zeros(logits.shape, jnp.int32)
    for k in range(TOP_K):
        rk = jnp.sum(jnp.where(lane == idxs[k], before, 0.0), axis=-1, keepdims=True)
        idx_o = jnp.where(lane == k, idxs[k], idx_o)
        gate_o = jnp.where(lane == k, es[k] / den, gate_o)
        rank_o = jnp.where(lane == k, rk.astype(jnp.int32), rank_o)
    idx_ref[...] = idx_o
    gate_ref[...] = gate_o
    rank_ref[...] = rank_o
    run_ref[...] += jnp.sum(onehot, axis=0, keepdims=True)
    cnt_ref[...] = jnp.broadcast_to(run_ref[...], cnt_ref.shape)


def _router(xs, g, mod, shift_idx, scale_idx, w_router, b_router, m, tr):
    T, D = xs.shape
    E = w_router.shape[1]
    wr = jnp.pad(w_router, ((0, 0), (0, LANES - E)))
    br = jnp.pad(b_router, (0, LANES - E)).reshape(1, LANES)
    slab = pl.BlockSpec((tr, LANES), lambda i: (i, 0))
    return pl.pallas_call(
        functools.partial(_router_kernel, m_tiles=m // tr, n_experts=E),
        grid=(T // tr,),
        in_specs=[
            pl.BlockSpec((tr, D), lambda i: (i, 0)),
            pl.BlockSpec((1, D), lambda i: (0, 0)),
            pl.BlockSpec((SUBLANES, D), lambda i: (0, shift_idx)),
            pl.BlockSpec((SUBLANES, D), lambda i: (0, scale_idx)),
            pl.BlockSpec((D, LANES), lambda i: (0, 0)),
            pl.BlockSpec((1, LANES), lambda i: (0, 0)),
        ],
        out_specs=[pl.BlockSpec((tr, D // 2), lambda i: (i, 0)), slab, slab, slab,
                   pl.BlockSpec((SUBLANES, LANES), lambda i: (0, 0))],
        out_shape=[jax.ShapeDtypeStruct((T, D // 2), jnp.uint32), jax.ShapeDtypeStruct((T, LANES), jnp.int32),
                   jax.ShapeDtypeStruct((T, LANES), F32), jax.ShapeDtypeStruct((T, LANES), jnp.int32),
                   jax.ShapeDtypeStruct((SUBLANES, LANES), F32)],
        scratch_shapes=[pltpu.VMEM((1, LANES), F32)],
        compiler_params=_params(("arbitrary",)),
        name="router",
    )(xs, g.reshape(1, D), mod, mod, wr, br)


def _dispatch_kernel(dest_ref, h_ref, xs_in_ref, xs_ref, sem):
    del xs_in_ref
    tr = h_ref.shape[0]

    def copy(r, k):
        return pltpu.make_async_copy(h_ref.at[pl.ds(r, 1), :], xs_ref.at[pl.ds(dest_ref[0, r * TOP_K + k], 1), :], sem)

    def issue(r, c):
        for k in range(TOP_K):
            copy(r, k).start()
        return c

    lax.fori_loop(0, tr, issue, 0)

    def drain(r, c):
        for k in range(TOP_K):
            copy(r, k).wait()
        return c

    lax.fori_loop(0, tr, drain, 0)


def _dispatch(h, dest, n_rows, tr):
    T, D = h.shape
    xs0 = jnp.zeros((n_rows, D), h.dtype)
    return pl.pallas_call(
        _dispatch_kernel,
        grid=(T // tr,),
        in_specs=[
            pl.BlockSpec((None, 1, tr * TOP_K), lambda i: (i, 0, 0), memory_space=pltpu.SMEM),
            pl.BlockSpec((tr, D), lambda i: (i, 0)),
            pl.BlockSpec(memory_space=pl.ANY),
        ],
        out_specs=pl.BlockSpec(memory_space=pl.ANY),
        out_shape=jax.ShapeDtypeStruct((n_rows, D), h.dtype),
        scratch_shapes=[pltpu.SemaphoreType.DMA(())],
        input_output_aliases={2: 0},
        compiler_params=_params(("arbitrary",)),
        name="dispatch",
    )(dest.reshape(T // tr, 1, tr * TOP_K), h, xs0)


def _deinterleave(gu):
    r, width = gu.shape
    lane = lax.broadcasted_iota(jnp.int32, (r, LANES), 1)
    idx_even = (2 * lane) % LANES
    idx_odd = idx_even + 1
    first = lane < LANES // 2
    even, odd = [], []
    for c in range(width // (2 * LANES)):
        xa = gu[:, 2 * LANES * c:2 * LANES * c + LANES]
        xb = gu[:, 2 * LANES * c + LANES:2 * LANES * (c + 1)]
        even.append(jnp.where(first, jnp.take_along_axis(xa, idx_even, axis=1), jnp.take_along_axis(xb, idx_even, axis=1)))
        odd.append(jnp.where(first, jnp.take_along_axis(xa, idx_odd, axis=1), jnp.take_along_axis(xb, idx_odd, axis=1)))
    return jnp.concatenate(even, axis=1), jnp.concatenate(odd, axis=1)


def _expert_kernel(be_ref, nu_ref, x_ref, wgu_ref, bgu_ref, wd_ref, bd_ref, o_ref):
    i = pl.program_id(0)

    @pl.when(i < nu_ref[0])
    def _():
        x_lo, x_hi = _unpack_rows(x_ref[...])
        x = jnp.concatenate([x_lo.astype(BF16), x_hi.astype(BF16)], axis=1)
        gu = jnp.dot(x, wgu_ref[...], preferred_element_type=F32) + bgu_ref[...]
        glu, lin = _deinterleave(gu)
        glu = jnp.minimum(glu, SWIGLU_LIMIT)
        lin = jnp.clip(lin, -SWIGLU_LIMIT, SWIGLU_LIMIT)
        act = glu * jax.nn.sigmoid(SWIGLU_ALPHA * glu) * (lin + 1.0)
        y = jnp.dot(act.astype(BF16), wd_ref[...], preferred_element_type=F32) + bd_ref[...]
        o_ref[...] = _pack_rows(y)

    @pl.when(i >= nu_ref[0])
    def _():
        o_ref[...] = jnp.zeros_like(o_ref)


def _experts(xs, block_e, n_used, wgu, bgu, wd, bd, l, bm):
    n_rows, Dw = xs.shape
    L, E, D, F2 = wgu.shape
    F = F2 // 2
    grid_spec = pltpu.PrefetchScalarGridSpec(
        num_scalar_prefetch=2,
        grid=(n_rows // bm,),
        in_specs=[
            pl.BlockSpec((bm, Dw), lambda i, be, nu: (i, 0)),
            pl.BlockSpec((None, None, D, F2), lambda i, be, nu: (l, be[i], 0, 0)),
            pl.BlockSpec((None, None, 1, F2), lambda i, be, nu: (l, be[i], 0, 0)),
            pl.BlockSpec((None, None, F, D), lambda i, be, nu: (l, be[i], 0, 0)),
            pl.BlockSpec((None, None, 1, D), lambda i, be, nu: (l, be[i], 0, 0)),
        ],
        out_specs=pl.BlockSpec((bm, Dw), lambda i, be, nu: (i, 0)),
    )
    return pl.pallas_call(
        _expert_kernel,
        grid_spec=grid_spec,
        out_shape=jax.ShapeDtypeStruct((n_rows, Dw), jnp.uint32),
        compiler_params=_params(("arbitrary",)),
        name="experts",
    )(block_e, n_used, xs, wgu, bgu.reshape(L, E, 1, F2), wd, bd.reshape(L, E, 1, D))


def _combine_kernel(dest_ref, ys_ref, gatew_ref, x_ref, mod_ref, o_ref, buf, sem, *, m_tiles):
    tr = x_ref.shape[0]
    is_ctx = pl.program_id(0) < m_tiles

    def copy(r, k):
        return pltpu.make_async_copy(ys_ref.at[pl.ds(dest_ref[0, r * TOP_K + k], 1), :], buf.at[k, pl.ds(r, 1), :], sem)

    def issue(r, c):
        for k in range(TOP_K):
            copy(r, k).start()
        return c

    lax.fori_loop(0, tr, issue, 0)

    def drain(r, c):
        for k in range(TOP_K):
            copy(r, k).wait()
        return c

    lax.fori_loop(0, tr, drain, 0)

    half = x_ref.shape[1] // 2
    g = jnp.where(is_ctx, mod_ref[1:2, :], mod_ref[0:1, :])

    def piece(t, c):
        rs = pl.ds(pl.multiple_of(t * SUBLANES, SUBLANES), SUBLANES)
        gw = gatew_ref[rs, :]
        f_lo = f_hi = None
        for k in range(TOP_K):
            lo, hi = _unpack_rows(buf[k, rs, :])
            f_lo = gw[:, k:k + 1] * lo if f_lo is None else f_lo + gw[:, k:k + 1] * lo
            f_hi = gw[:, k:k + 1] * hi if f_hi is None else f_hi + gw[:, k:k + 1] * hi
        o_ref[rs, 0:half] = x_ref[rs, 0:half] + g[:, 0:half] * f_lo
        o_ref[rs, half:] = x_ref[rs, half:] + g[:, half:] * f_hi
        return c

    lax.fori_loop(0, tr // SUBLANES, piece, 0)


def _combine(ys, dest, gatew, xs, mod, gate_idx, m, tr):
    T, D = xs.shape
    return pl.pallas_call(
        functools.partial(_combine_kernel, m_tiles=m // tr),
        grid=(T // tr,),
        in_specs=[
            pl.BlockSpec((None, 1, tr * TOP_K), lambda i: (i, 0, 0), memory_space=pltpu.SMEM),
            pl.BlockSpec(memory_space=pl.ANY),
            pl.BlockSpec((tr, LANES), lambda i: (i, 0)),
            pl.BlockSpec((tr, D), lambda i: (i, 0)),
            pl.BlockSpec((SUBLANES, D), lambda i: (0, gate_idx)),
        ],
        out_specs=pl.BlockSpec((tr, D), lambda i: (i, 0)),
        out_shape=jax.ShapeDtypeStruct((T, D), F32),
        scratch_shapes=[pltpu.VMEM((TOP_K, tr, D // 2), jnp.uint32), pltpu.SemaphoreType.DMA(())],
        compiler_params=_params(("arbitrary",)),
        name="combine",
    )(dest.reshape(T // tr, 1, tr * TOP_K), ys, gatew, xs, mod)


def _final_norm_kernel(x_ref, g_ref, o_ref):
    x = x_ref[...]
    o_ref[...] = x * lax.rsqrt(jnp.mean(x * x, axis=-1, keepdims=True) + NORM_EPS) * g_ref[...]


def _final_norm(xs, g, m, n, tr):
    D = xs.shape[1]
    off = m // tr
    return pl.pallas_call(
        _final_norm_kernel,
        grid=(n // tr,),
        in_specs=[pl.BlockSpec((tr, D), lambda i: (i + off, 0)), pl.BlockSpec((1, D), lambda i: (0, 0))],
        out_specs=pl.BlockSpec((tr, D), lambda i: (i, 0)),
        out_shape=jax.ShapeDtypeStruct((n, D), F32),
        compiler_params=_params(("parallel",)),
        name="final_norm",
    )(xs, g.reshape(1, D))


def kernel(x, c, ctx, c_ctx, w_ada, b_ada, norm1_g, norm2_g, w_in, lam_qk, subln_g, conv_w, conv_b, lru_gate_w,
           lru_gate_b, lru_lambda, na_rpb, w_branch, w_out, w_router, b_router, w_gu, b_gu, w_down, b_down, final_g):
    B, n, D = x.shape
    m = ctx.shape[1]
    assert B == 1 and n % (NA_QROWS * GRID_W) == 0
    L = w_ada.shape[0]
    E = w_router.shape[-1]
    T = m + n
    rows = n // GRID_W
    tr = 256
    assert m % tr == 0 and T % tr == 0 and m % (NA_QROWS * GRID_W) == 0
    tm = _tile(T, 1408, 128)
    tn = _tile(D, 512, LANES)
    bm = 256
    n_blocks = -(-(T * TOP_K + E * (bm - 1)) // bm)
    n_rows = n_blocks * bm

    xs = jnp.concatenate([ctx[0], x[0]], axis=0)
    mod_all = _ada(c, c_ctx, w_ada, b_ada)
    cos, sin = _rope_tables(m, n)
    w_branch_b, w_out_b, w_gu_b, w_down_b = (w.astype(BF16) for w in (w_branch, w_out, w_gu, w_down))

    for l in range(L):
        lambda_init = 0.8 - 0.6 * math.exp(-0.3 * l)
        mod = mod_all[l]
        h = _norm_mod(xs, norm1_g[l], mod, 0, 1, m, tr)
        z = _matmul(h, w_in, l, tm, tn, BF16)
        qk = _rope(z, cos, sin, tr)
        o_a = _attn_a(qk, z, lam_qk[l], subln_g[l], lambda_init, m, 256, 1536)
        o_b = _rglru(z, conv_w[l], conv_b[l], lru_gate_w[l], lru_gate_b[l], lru_lambda[l], m, 256)
        o_c = _na(z, na_rpb[l], m)
        y = _merge(o_a, o_b, o_c, w_branch_b, l, z, D, tm, tn)
        xs = _outproj(y, w_out_b, l, xs, mod, 2, m, tm, tn)

        h2, idx, gatew, rank, cnt = _router(xs, norm2_g[l], mod, 3, 4, w_router[l], b_router[l], m, tr)
        counts = cnt[0, :E].astype(jnp.int32)
        padded = (counts + bm - 1) // bm * bm
        pend = jnp.cumsum(padded)
        pstart = pend - padded
        dest = (pstart[idx[:, :TOP_K]] + rank[:, :TOP_K]).astype(jnp.int32)
        block_start = jnp.arange(n_blocks, dtype=jnp.int32) * bm
        block_e = jnp.minimum(jnp.sum((pend[None, :] <= block_start[:, None]).astype(jnp.int32), axis=1), E - 1)
        n_used = (pend[-1:] // bm).astype(jnp.int32)

        xg = _dispatch(h2, dest, n_rows, tr)
        ys = _experts(xg, block_e, n_used, w_gu_b, b_gu, w_down_b, b_down, l, bm)
        xs = _combine(ys, dest, gatew, xs, mod, 5, m, tr)

    return _final_norm(xs, final_g, m, n, tr)[None]
```

```python
import functools
import math

import numpy as np
import jax
import jax.numpy as jnp
from jax import lax
from jax.experimental import pallas as pl
from jax.experimental.pallas import tpu as pltpu

F32 = jnp.float32
BF16 = jnp.bfloat16

GRID_W = 64
NORM_EPS = 1e-6
A_HEADS, A_DH = 8, 64
ROPE_PAIRS = A_DH // 4
ROPE_BASE = 10000.0
B_WIDTH, B_BW, B_CONV, CONV_LEFT, LRU_C = 1024, 128, 4, 2, 8.0
C_HEADS, C_DH, NA_ROWS, NA_COLS = 16, 64, 8, 16
N_BRANCH, BRANCH_W = 3, 1024
TOP_K = 4
SWIGLU_ALPHA, SWIGLU_LIMIT = 1.702, 7.0

LANES = 128
SUBLANES = 8
COL_AQ, COL_AK, COL_AV, COL_BX, COL_BY, COL_CQ, COL_CK, COL_CV, COL_G = 0, 8, 16, 24, 32, 40, 48, 56, 64
NEG_BIG = -1e30
VMEM_LIMIT = 56 * 1024 * 1024


def _params(sem):
    return pltpu.CompilerParams(dimension_semantics=sem, vmem_limit_bytes=VMEM_LIMIT)


def _tile(dim, target, mult):
    best = None
    for t in range(mult, min(dim, target) + 1, mult):
        if dim % t == 0:
            best = t
    assert best is not None, (dim, target, mult)
    return best


def _ada_kernel(c_ref, w_ref, b_ref, o_ref, acc_ref):
    k = pl.program_id(2)

    @pl.when(k == 0)
    def _():
        acc_ref[...] = jnp.zeros_like(acc_ref)

    cc = c_ref[...]
    s = cc * jax.nn.sigmoid(cc)
    hi = s.astype(BF16)
    lo = (s - hi.astype(F32)).astype(BF16)
    row = lax.broadcasted_iota(jnp.int32, s.shape, 0)
    lhs = jnp.where(row < 2, hi, lo)
    acc_ref[...] += jnp.dot(lhs, w_ref[...].astype(BF16), preferred_element_type=F32)

    @pl.when(k == pl.num_programs(2) - 1)
    def _():
        acc = acc_ref[...]
        o_ref[...] = acc + pltpu.roll(acc, SUBLANES - 2, 0) + b_ref[...]


def _ada(c, c_ctx, w_ada, b_ada):
    L, D, N = w_ada.shape
    cc = jnp.concatenate([c[:1], c_ctx[None], c[:1], c_ctx[None], jnp.zeros((4, D), F32)], axis=0)
    tk = _tile(D, 1024, LANES)
    tn = _tile(N, 2048, LANES)
    return pl.pallas_call(
        _ada_kernel,
        grid=(L, N // tn, D // tk),
        in_specs=[
            pl.BlockSpec((SUBLANES, tk), lambda l, j, k: (0, k)),
            pl.BlockSpec((None, tk, tn), lambda l, j, k: (l, k, j)),
            pl.BlockSpec((None, 1, tn), lambda l, j, k: (l, 0, j)),
        ],
        out_specs=pl.BlockSpec((None, SUBLANES, tn), lambda l, j, k: (l, 0, j)),
        out_shape=jax.ShapeDtypeStruct((L, SUBLANES, N), F32),
        scratch_shapes=[pltpu.VMEM((SUBLANES, tn), F32)],
        compiler_params=_params(("parallel", "parallel", "arbitrary")),
        name="ada",
    )(cc, w_ada, b_ada.reshape(L, 1, N))


def _norm_mod_kernel(x_ref, g_ref, sh_ref, sc_ref, o_ref, *, m_tiles):
    is_ctx = pl.program_id(0) < m_tiles
    x = x_ref[...]
    y = x * lax.rsqrt(jnp.mean(x * x, axis=-1, keepdims=True) + NORM_EPS) * g_ref[...]
    sh = jnp.where(is_ctx, sh_ref[1:2, :], sh_ref[0:1, :])
    sc = jnp.where(is_ctx, sc_ref[1:2, :], sc_ref[0:1, :])
    o_ref[...] = (y * (1.0 + sc) + sh).astype(o_ref.dtype)


def _norm_mod(xs, g, mod, shift_idx, scale_idx, m, tr):
    T, D = xs.shape
    return pl.pallas_call(
        functools.partial(_norm_mod_kernel, m_tiles=m // tr),
        grid=(T // tr,),
        in_specs=[
            pl.BlockSpec((tr, D), lambda i: (i, 0)),
            pl.BlockSpec((1, D), lambda i: (0, 0)),
            pl.BlockSpec((SUBLANES, D), lambda i: (0, shift_idx)),
            pl.BlockSpec((SUBLANES, D), lambda i: (0, scale_idx)),
        ],
        out_specs=pl.BlockSpec((tr, D), lambda i: (i, 0)),
        out_shape=jax.ShapeDtypeStruct((T, D), BF16),
        compiler_params=_params(("parallel",)),
        name="norm_mod",
    )(xs, g.reshape(1, D), mod, mod)


def _mm_kernel(a_ref, b_ref, o_ref):
    o_ref[...] = jnp.dot(a_ref[...], b_ref[...].astype(BF16), preferred_element_type=F32).astype(o_ref.dtype)


def _matmul(a, b_all, l, tm, tn, out_dtype):
    M, K = a.shape
    N = b_all.shape[-1]
    return pl.pallas_call(
        _mm_kernel,
        grid=(M // tm, N // tn),
        in_specs=[pl.BlockSpec((tm, K), lambda i, j: (i, 0)), pl.BlockSpec((None, K, tn), lambda i, j: (l, 0, j))],
        out_specs=pl.BlockSpec((tm, tn), lambda i, j: (i, j)),
        out_shape=jax.ShapeDtypeStruct((M, N), out_dtype),
        compiler_params=_params(("parallel", "arbitrary")),
        name="in_proj",
    )(a, b_all)


def _rope_tables(m, n):
    t = np.arange(n)
    row = (t // GRID_W).astype(np.float32)
    col = (t % GRID_W).astype(np.float32)
    inv = jnp.asarray(ROPE_BASE, F32) ** (-jnp.arange(ROPE_PAIRS, dtype=F32) / ROPE_PAIRS)
    ang_r = jnp.asarray(row)[:, None] * inv
    ang_c = jnp.asarray(col)[:, None] * inv
    lane = np.arange(LANES)
    d = lane % A_DH
    use_col = (d // (A_DH // 2)) == 1
    within = d % (A_DH // 2)
    pair = within % ROPE_PAIRS
    sign = np.where(within < ROPE_PAIRS, -1.0, 1.0).astype(np.float32)
    ang = jnp.where(jnp.asarray(use_col)[None, :], ang_c[:, pair], ang_r[:, pair])
    cos = jnp.concatenate([jnp.ones((m, LANES), F32), jnp.cos(ang)], axis=0)
    sin = jnp.concatenate([jnp.zeros((m, LANES), F32), jnp.sin(ang) * jnp.asarray(sign)[None, :]], axis=0)
    return cos, sin


def _rope_kernel(z_ref, cos_ref, sin_ref, o_ref):
    scale = jnp.where(pl.program_id(1) == 0, A_DH ** -0.5, 1.0).astype(F32)
    cos = cos_ref[...]
    sin = sin_ref[...]
    lane = lax.broadcasted_iota(jnp.int32, cos.shape, 1)
    lo = (lane % (2 * ROPE_PAIRS)) < ROPE_PAIRS
    for hc in range(z_ref.shape[1] // LANES):
        x = z_ref[:, hc * LANES:(hc + 1) * LANES].astype(F32)
        partner = jnp.where(lo, pltpu.roll(x, LANES - ROPE_PAIRS, 1), pltpu.roll(x, ROPE_PAIRS, 1))
        o_ref[:, hc * LANES:(hc + 1) * LANES] = ((x * cos + partner * sin) * scale).astype(o_ref.dtype)


def _rope(z, cos, sin, tr):
    T = z.shape[0]
    W = A_HEADS * 2 * A_DH
    return pl.pallas_call(
        _rope_kernel,
        grid=(T // tr, 2),
        in_specs=[
            pl.BlockSpec((tr, W), lambda i, j: (i, j)),
            pl.BlockSpec((tr, LANES), lambda i, j: (i, 0)),
            pl.BlockSpec((tr, LANES), lambda i, j: (i, 0)),
        ],
        out_specs=pl.BlockSpec((tr, W), lambda i, j: (i, j)),
        out_shape=jax.ShapeDtypeStruct((T, 2 * W), BF16),
        compiler_params=_params(("parallel", "arbitrary")),
        name="rope",
    )(z, cos, sin)


def _attn_a_kernel(lam_ref, g_ref, q_ref, k_ref, v_ref, o_ref, *, m, T, tk, lambda_init):
    qi = pl.program_id(1)
    qb = q_ref[...]
    tq = qb.shape[0]
    lane = lax.broadcasted_iota(jnp.int32, qb.shape, 1)
    zero = jnp.zeros_like(qb)
    qs = jnp.concatenate([jnp.where(lane < A_DH, qb, zero), jnp.where(lane >= A_DH, qb, zero)], axis=0)

    def scores(kc):
        return lax.dot_general(qs, kc, (((1,), (1,)), ((), ())), preferred_element_type=F32)

    def step(s, vc, carry):
        m_i, l_i, acc = carry
        m_new = jnp.maximum(m_i, jnp.max(s, axis=-1, keepdims=True))
        alpha = jnp.exp(m_i - m_new)
        p = jnp.exp(s - m_new)
        l_new = alpha * l_i + jnp.sum(p, axis=-1, keepdims=True)
        acc = alpha * acc + jnp.dot(p.astype(BF16), vc, preferred_element_type=F32)
        return m_new, l_new, acc

    init = (jnp.full((2 * tq, 1), NEG_BIG, F32), jnp.zeros((2 * tq, 1), F32), jnp.zeros((2 * tq, LANES), F32))

    lq = lam_ref[...]
    lam = (jnp.exp(jnp.sum(lq[0:1] * lq[1:2], axis=-1, keepdims=True))
           - jnp.exp(jnp.sum(lq[2:3] * lq[3:4], axis=-1, keepdims=True)) + lambda_init)

    def finish(carry):
        _, l_i, acc = carry
        o = acc / l_i
        d = o[:tq] - lam * o[tq:]
        y = d * lax.rsqrt(jnp.mean(d * d, axis=-1, keepdims=True) + NORM_EPS) * g_ref[...]
        o_ref[...] = (y * (1.0 - lambda_init)).astype(o_ref.dtype)

    @pl.when(qi < m // tq)
    def _():
        finish(step(scores(k_ref[0:m, :]), v_ref[0:m, :], init))

    @pl.when(qi >= m // tq)
    def _():
        bounds = [(lo, min(lo + tk, T)) for lo in range(0, T, tk)]
        carry = init
        s = scores(k_ref[bounds[0][0]:bounds[0][1], :])
        for c, (lo, hi) in enumerate(bounds):
            s_next = scores(k_ref[bounds[c + 1][0]:bounds[c + 1][1], :]) if c + 1 < len(bounds) else None
            carry = step(s, v_ref[lo:hi, :], carry)
            s = s_next
        finish(carry)


def _attn_a(qk, z, lam_qk, subln_g, lambda_init, m, tq, tk):
    T = z.shape[0]
    return pl.pallas_call(
        functools.partial(_attn_a_kernel, m=m, T=T, tk=tk, lambda_init=lambda_init),
        grid=(A_HEADS, T // tq),
        in_specs=[
            pl.BlockSpec((4, A_DH), lambda h, i: (0, 0)),
            pl.BlockSpec((1, 2 * A_DH), lambda h, i: (0, 0)),
            pl.BlockSpec((tq, LANES), lambda h, i: (i, h)),
            pl.BlockSpec((T, LANES), lambda h, i: (0, A_HEADS + h)),
            pl.BlockSpec((T, LANES), lambda h, i: (0, COL_AV + h)),
        ],
        out_specs=pl.BlockSpec((tq, LANES), lambda h, i: (i, h)),
        out_shape=jax.ShapeDtypeStruct((T, A_HEADS * 2 * A_DH), BF16),
        compiler_params=_params(("parallel", "arbitrary")),
        name="diff_attn",
    )(lam_qk, subln_g.reshape(1, 2 * A_DH), qk, qk, z)


def _rglru_kernel(bx_ref, by_ref, cw_ref, cb_ref, gw_ref, gb_ref, lam_ref, o_ref, xpad, hf, *, m, T, tc):
    nchunk = T // tc
    mc = m // tc
    nb = tc // SUBLANES
    halo = SUBLANES
    xpad[0:halo, :] = jnp.zeros((halo, LANES), F32)
    xpad[halo + T:2 * halo + T, :] = jnp.zeros((halo, LANES), F32)
    xpad[halo:halo + T, :] = bx_ref[...].astype(F32)

    row = lax.broadcasted_iota(jnp.int32, (tc, LANES), 0)
    sub = lax.broadcasted_iota(jnp.int32, (nb, SUBLANES, LANES), 1)
    cw = cw_ref[...]
    cb = cb_ref[...]

    def coeffs(j, d):
        r0 = j * tc
        left_cut = r0 == m
        right_cut = (r0 + tc) == m
        x_m2 = xpad[pl.ds(r0 + halo - 2, tc), :]
        x_m1 = xpad[pl.ds(r0 + halo - 1, tc), :]
        x_0 = xpad[pl.ds(r0 + halo, tc), :]
        x_p1 = xpad[pl.ds(r0 + halo + 1, tc), :]
        x_m2 = jnp.where(jnp.logical_and(left_cut, row < 2), 0.0, x_m2)
        x_m1 = jnp.where(jnp.logical_and(left_cut, row < 1), 0.0, x_m1)
        x_p1 = jnp.where(jnp.logical_and(right_cut, row >= tc - 1), 0.0, x_p1)
        u = cb + x_m2 * cw[0:1] + x_m1 * cw[1:2] + x_0 * cw[2:3] + x_p1 * cw[3:4]
        ub = u.astype(BF16)
        gr = jnp.dot(ub, gw_ref[d, 0].astype(BF16), preferred_element_type=F32) + gb_ref[d, 0]
        gi = jnp.dot(ub, gw_ref[d, 1].astype(BF16), preferred_element_type=F32) + gb_ref[d, 1]
        r = jax.nn.sigmoid(gr)
        i = jax.nn.sigmoid(gi)
        log_a = (-LRU_C * jax.nn.softplus(-lam_ref[d])) * r
        a = jnp.exp(log_a)
        b = jnp.sqrt(-jnp.tanh(log_a) * (a * a + 1.0)) * (i * u)
        return a, b

    def chunk_scan(a, b, h, rev):
        A = a.reshape(nb, SUBLANES, LANES)
        B = b.reshape(nb, SUBLANES, LANES)
        for s in (1, 2, 4):
            shift = SUBLANES - s if rev else s
            msk = (sub < SUBLANES - s) if rev else (sub >= s)
            Ap = pltpu.roll(A, shift, 1)
            Bp = pltpu.roll(B, shift, 1)
            B = jnp.where(msk, A * Bp + B, B)
            A = jnp.where(msk, A * Ap, A)
        outs = [None] * nb
        order = range(nb - 1, -1, -1) if rev else range(nb)
        for jb in order:
            hb = A[jb] * h + B[jb]
            outs[jb] = hb
            edge = hb[0:1, :] if rev else hb[SUBLANES - 1:SUBLANES, :]
            h = jnp.broadcast_to(edge, (SUBLANES, LANES))
        return jnp.concatenate(outs, axis=0), h

    def fwd_body(j, h):
        a, b = coeffs(j, 0)
        hc, h = chunk_scan(a, b, h, False)
        hf[pl.ds(pl.multiple_of(j * tc, tc), tc), :] = hc
        return h

    lax.fori_loop(0, nchunk, fwd_body, jnp.zeros((SUBLANES, LANES), F32))

    def bwd_body(s, h):
        j = jnp.where(s < mc, mc - 1 - s, nchunk - 1 - (s - mc))
        a, b = coeffs(j, 1)
        hc, h = chunk_scan(a, b, h, True)
        r0 = pl.multiple_of(j * tc, tc)
        by = by_ref[pl.ds(r0, tc), :].astype(F32)
        o_ref[pl.ds(r0, tc), :] = ((hf[pl.ds(r0, tc), :] + hc) * jax.nn.gelu(by)).astype(o_ref.dtype)
        return h

    lax.fori_loop(0, nchunk, bwd_body, jnp.zeros((SUBLANES, LANES), F32))


def _rglru(z, conv_w, conv_b, gate_w, gate_b, lru_lambda, m, tc):
    T = z.shape[0]
    nblk = B_WIDTH // B_BW
    return pl.pallas_call(
        functools.partial(_rglru_kernel, m=m, T=T, tc=tc),
        grid=(nblk,),
        in_specs=[
            pl.BlockSpec((T, LANES), lambda c: (0, COL_BX + c)),
            pl.BlockSpec((T, LANES), lambda c: (0, COL_BY + c)),
            pl.BlockSpec((B_CONV, LANES), lambda c: (0, c)),
            pl.BlockSpec((1, LANES), lambda c: (0, c)),
            pl.BlockSpec((2, 2, None, B_BW, B_BW), lambda c: (0, 0, c, 0, 0)),
            pl.BlockSpec((2, 2, None, 1, B_BW), lambda c: (0, 0, c, 0, 0)),
            pl.BlockSpec((2, 1, LANES), lambda c: (0, 0, c)),
        ],
        out_specs=pl.BlockSpec((T, LANES), lambda c: (0, c)),
        out_shape=jax.ShapeDtypeStruct((T, B_WIDTH), BF16),
        scratch_shapes=[pltpu.VMEM((T + 2 * SUBLANES, LANES), F32), pltpu.VMEM((T, LANES), F32)],
        compiler_params=_params(("parallel",)),
        name="rglru",
    )(z, z, conv_w, conv_b.reshape(1, B_WIDTH), gate_w, gate_b.reshape(2, 2, nblk, 1, B_BW),
      lru_lambda.reshape(2, 1, B_WIDTH))


NA_QROWS = 4
NA_KROWS = NA_QROWS + NA_ROWS - 1


def _na_base(g, rows):
    return jnp.clip(NA_QROWS * g - NA_ROWS // 2, 0, rows - NA_KROWS)


def _na_patterns(rows):
    sigs, reps, pat_of_g = {}, [], []
    for g in range(rows // NA_QROWS):
        base = int(np.clip(NA_QROWS * g - NA_ROWS // 2, 0, rows - NA_KROWS))
        sig = (base - NA_QROWS * g,) + tuple(
            int(np.clip(r - NA_ROWS // 2, 0, rows - NA_ROWS)) - r for r in range(NA_QROWS * g, NA_QROWS * (g + 1)))
        if sig not in sigs:
            sigs[sig] = len(reps)
            reps.append(g)
        pat_of_g.append(sigs[sig])
    return reps, pat_of_g


def _na_bias(rpb, rows, reps):
    n_rr, n_cr = 2 * NA_ROWS - 1, 2 * NA_COLS - 1
    sel_r = np.zeros((len(reps), NA_QROWS, NA_KROWS, n_rr), np.float32)
    for p, g in enumerate(reps):
        base = int(np.clip(NA_QROWS * g - NA_ROWS // 2, 0, rows - NA_KROWS))
        for rho in range(NA_QROWS):
            r = NA_QROWS * g + rho
            r0 = int(np.clip(r - NA_ROWS // 2, 0, rows - NA_ROWS))
            for j in range(NA_KROWS):
                if r0 <= base + j < r0 + NA_ROWS:
                    sel_r[p, rho, j, base + j - r + NA_ROWS - 1] = 1.0
    sel_c = np.zeros((GRID_W, GRID_W, n_cr), np.float32)
    for c in range(GRID_W):
        c0 = int(np.clip(c - NA_COLS // 2, 0, GRID_W - NA_COLS))
        for kc in range(c0, c0 + NA_COLS):
            sel_c[c, kc, kc - c + NA_COLS - 1] = 1.0
    ok = (sel_r.sum(-1) > 0)[:, :, None, :, None] & (sel_c.sum(-1) > 0)[None, None, :, None, :]
    hp = lax.Precision.HIGHEST
    t = jnp.einsum('prjR,hRC->hprjC', jnp.asarray(sel_r), rpb.astype(F32), precision=hp)
    vals = jnp.einsum('hprjC,ckC->phrcjk', t, jnp.asarray(sel_c), precision=hp)
    bias = jnp.where(jnp.asarray(ok)[:, None], vals, NEG_BIG)
    return bias.reshape(len(reps), rpb.shape[0], NA_QROWS * GRID_W, NA_KROWS * GRID_W)


def _na_kernel(pat_ref, q_ref, k_ref, v_ref, bias_ref, o_ref, *, m, rows):
    tq = NA_QROWS * GRID_W
    nk = NA_KROWS * GRID_W
    lane = lax.broadcasted_iota(jnp.int32, (tq, LANES), 1)
    nt = (((1,), (1,)), ((), ()))

    def stacked_q(row0):
        qb = q_ref[pl.ds(row0, tq), :] * jnp.asarray(C_DH ** -0.5, BF16)
        zero = jnp.zeros_like(qb)
        return jnp.concatenate([jnp.where(lane < C_DH, qb, zero), jnp.where(lane >= C_DH, qb, zero)], axis=0)

    def emit(row0, o):
        o_ref[pl.ds(row0, tq), :] = jnp.where(lane < C_DH, o[:tq], o[tq:]).astype(o_ref.dtype)

    def softmax_pv(parts):
        mx = None
        for s, _ in parts:
            pm = jnp.max(s, axis=-1, keepdims=True)
            mx = pm if mx is None else jnp.maximum(mx, pm)
        l = None
        o = None
        for s, v in parts:
            p = jnp.exp(s - mx)
            ps = jnp.sum(p, axis=-1, keepdims=True)
            po = jnp.dot(p.astype(BF16), v, preferred_element_type=F32)
            l = ps if l is None else l + ps
            o = po if o is None else o + po
        return o / l

    for c in range(m // tq):
        emit(c * tq, softmax_pv([(lax.dot_general(stacked_q(c * tq), k_ref[0:m, :], nt, preferred_element_type=F32),
                                  v_ref[0:m, :])]))

    def group(g, carry):
        row0 = pl.multiple_of(m + g * tq, tq)
        off = pl.multiple_of(m + _na_base(g, rows) * GRID_W, GRID_W)
        qs = stacked_q(row0)
        bias = bias_ref[pat_ref[g]].reshape(2 * tq, nk)
        s_nb = lax.dot_general(qs, k_ref[pl.ds(off, nk), :], nt, preferred_element_type=F32) + bias
        s_cx = lax.dot_general(qs, k_ref[0:m, :], nt, preferred_element_type=F32)
        emit(row0, softmax_pv([(s_nb, v_ref[pl.ds(off, nk), :]), (s_cx, v_ref[0:m, :])]))
        return carry

    lax.fori_loop(0, rows // NA_QROWS, group, 0, unroll=2)


def _na(z, rpb, m):
    T = z.shape[0]
    rows = (T - m) // GRID_W
    tq = NA_QROWS * GRID_W
    assert m % tq == 0 and (rows // NA_QROWS) % 2 == 0
    reps, pat_of_g = _na_patterns(rows)
    bias = _na_bias(rpb, rows, reps)
    pat = jnp.asarray(pat_of_g, jnp.int32)
    col = lambda base: pl.BlockSpec((T, LANES), lambda h, pat: (0, base + h))
    grid_spec = pltpu.PrefetchScalarGridSpec(
        num_scalar_prefetch=1,
        grid=(C_HEADS // 2,),
        in_specs=[col(COL_CQ), col(COL_CK), col(COL_CV),
                  pl.BlockSpec((len(reps), 2, tq, NA_KROWS * GRID_W), lambda h, pat: (0, h, 0, 0))],
        out_specs=pl.BlockSpec((T, LANES), lambda h, pat: (0, h)),
    )
    return pl.pallas_call(
        functools.partial(_na_kernel, m=m, rows=rows),
        grid_spec=grid_spec,
        out_shape=jax.ShapeDtypeStruct((T, C_HEADS * C_DH), BF16),
        compiler_params=_params(("parallel",)),
        name="nbr_attn",
    )(pat, z, z, z, bias)


def _merge_kernel(oa_ref, ob_ref, oc_ref, wb_ref, g0_ref, g1_ref, g2_ref, o_ref):
    acc = None
    for i, (o, g) in enumerate(((oa_ref, g0_ref), (ob_ref, g1_ref), (oc_ref, g2_ref))):
        t = jax.nn.sigmoid(g[...].astype(F32)) * jnp.dot(o[...], wb_ref[i], preferred_element_type=F32)
        acc = t if acc is None else acc + t
    o_ref[...] = acc.astype(o_ref.dtype)


def _merge(oa, ob, oc, wb_all, l, z, D, tm, tn):
    T = oa.shape[0]
    gcol = COL_G * LANES // tn
    nd = D // tn
    o_spec = pl.BlockSpec((tm, BRANCH_W), lambda i, j: (i, 0))
    return pl.pallas_call(
        _merge_kernel,
        grid=(T // tm, D // tn),
        in_specs=[o_spec, o_spec, o_spec,
                  pl.BlockSpec((None, N_BRANCH, BRANCH_W, tn), lambda i, j: (l, 0, 0, j)),
                  pl.BlockSpec((tm, tn), lambda i, j: (i, gcol + j)),
                  pl.BlockSpec((tm, tn), lambda i, j: (i, gcol + nd + j)),
                  pl.BlockSpec((tm, tn), lambda i, j: (i, gcol + 2 * nd + j))],
        out_specs=pl.BlockSpec((tm, tn), lambda i, j: (i, j)),
        out_shape=jax.ShapeDtypeStruct((T, D), BF16),
        compiler_params=_params(("parallel", "arbitrary")),
        name="merge",
    )(oa, ob, oc, wb_all, z, z, z)


def _outproj_kernel(y_ref, w_ref, x_ref, gate_ref, o_ref, *, m):
    tm = y_ref.shape[0]
    r = jnp.dot(y_ref[...], w_ref[...], preferred_element_type=F32)
    row = pl.program_id(0) * tm + lax.broadcasted_iota(jnp.int32, r.shape, 0)
    g = jnp.where(row < m, gate_ref[1:2, :], gate_ref[0:1, :])
    o_ref[...] = x_ref[...] + g * r


def _outproj(y, w_all, l, xs, mod, gate_idx, m, tm, tn):
    T, D = xs.shape
    nd = D // tn
    return pl.pallas_call(
        functools.partial(_outproj_kernel, m=m),
        grid=(T // tm, nd),
        in_specs=[pl.BlockSpec((tm, D), lambda i, j: (i, 0)),
                  pl.BlockSpec((None, D, tn), lambda i, j: (l, 0, j)),
                  pl.BlockSpec((tm, tn), lambda i, j: (i, j)),
                  pl.BlockSpec((SUBLANES, tn), lambda i, j: (0, gate_idx * nd + j))],
        out_specs=pl.BlockSpec((tm, tn), lambda i, j: (i, j)),
        out_shape=jax.ShapeDtypeStruct((T, D), F32),
        compiler_params=_params(("parallel", "arbitrary")),
        name="out_proj",
    )(y, w_all, xs, mod)


HI_MASK = 0xFFFF0000


def _pack_rows(v):
    half = v.shape[1] // 2
    u = lax.bitcast_convert_type(v.astype(BF16).astype(F32), jnp.uint32)
    return (u[:, half:] & jnp.uint32(HI_MASK)) | (u[:, :half] >> 16)


def _unpack_rows(w):
    lo = lax.bitcast_convert_type(w << 16, F32)
    hi = lax.bitcast_convert_type(w & jnp.uint32(HI_MASK), F32)
    return lo, hi

def _router_kernel(x_ref, g_ref, sh_ref, sc_ref, wr_ref, br_ref, h_ref, idx_ref, gate_ref, rank_ref, cnt_ref,
                   run_ref, *, m_tiles, n_experts):
    i = pl.program_id(0)
    is_ctx = i < m_tiles

    @pl.when(i == 0)
    def _():
        run_ref[...] = jnp.zeros_like(run_ref)

    x = x_ref[...]
    tr = x.shape[0]
    y = x * lax.rsqrt(jnp.mean(x * x, axis=-1, keepdims=True) + NORM_EPS) * g_ref[...]
    sh = jnp.where(is_ctx, sh_ref[1:2, :], sh_ref[0:1, :])
    sc = jnp.where(is_ctx, sc_ref[1:2, :], sc_ref[0:1, :])
    h = y * (1.0 + sc) + sh
    hb = h.astype(BF16)
    h_ref[...] = _pack_rows(h)

    h_lo = (h - hb.astype(F32)).astype(BF16)
    w = wr_ref[...]
    w_hi = w.astype(BF16)
    w_lo = (w - w_hi.astype(F32)).astype(BF16)
    logits = (jnp.dot(hb, w_hi, preferred_element_type=F32) + jnp.dot(h_lo, w_hi, preferred_element_type=F32)
              + jnp.dot(hb, w_lo, preferred_element_type=F32)) + br_ref[...]
    lane = lax.broadcasted_iota(jnp.int32, logits.shape, 1)
    work = jnp.where(lane < n_experts, logits, NEG_BIG)

    vals, idxs = [], []
    for _ in range(TOP_K):
        mx = jnp.max(work, axis=-1, keepdims=True)
        ix = jnp.min(jnp.where(work == mx, lane, LANES), axis=-1, keepdims=True)
        vals.append(mx)
        idxs.append(ix)
        work = jnp.where(lane == ix, NEG_BIG, work)
    es = [jnp.exp(v - vals[0]) for v in vals]
    den = es[0] + es[1] + es[2] + es[3]

    onehot = jnp.zeros(logits.shape, F32)
    for ix in idxs:
        onehot = onehot + jnp.where(lane == ix, 1.0, 0.0)
    r_i = lax.broadcasted_iota(jnp.int32, (tr, tr), 0)
    c_i = lax.broadcasted_iota(jnp.int32, (tr, tr), 1)
    tri = jnp.where(c_i < r_i, 1.0, 0.0).astype(BF16)
    before = jnp.dot(tri, onehot.astype(BF16), preferred_element_type=F32) + run_ref[...]

    idx_o = jnp.zeros(logits.shape, jnp.int32)
    gate_o = jnp.zeros(logits.shape, F32)
    rank_o = jnp.zeros(logits.shape, jnp.int32)
    for k in range(TOP_K):
        rk = jnp.sum(jnp.where(lane == idxs[k], before, 0.0), axis=-1, keepdims=True)
        idx_o = jnp.where(lane == k, idxs[k], idx_o)
        gate_o = jnp.where(lane == k, es[k] / den, gate_o)
        rank_o = jnp.where(lane == k, rk.astype(jnp.int32), rank_o)
    idx_ref[...] = idx_o
    gate_ref[...] = gate_o
    rank_ref[...] = rank_o
    run_ref[...] += jnp.sum(onehot, axis=0, keepdims=True)
    cnt_ref[...] = jnp.broadcast_to(run_ref[...], cnt_ref.shape)


def _router(xs, g, mod, shift_idx, scale_idx, w_router, b_router, m, tr):
    T, D = xs.shape
    E = w_router.shape[1]
    wr = jnp.pad(w_router, ((0, 0), (0, LANES - E)))
    br = jnp.pad(b_router, (0, LANES - E)).reshape(1, LANES)
    slab = pl.BlockSpec((tr, LANES), lambda i: (i, 0))
    return pl.pallas_call(
        functools.partial(_router_kernel, m_tiles=m // tr, n_experts=E),
        grid=(T // tr,),
        in_specs=[
            pl.BlockSpec((tr, D), lambda i: (i, 0)),
            pl.BlockSpec((1, D), lambda i: (0, 0)),
            pl.BlockSpec((SUBLANES, D), lambda i: (0, shift_idx)),
            pl.BlockSpec((SUBLANES, D), lambda i: (0, scale_idx)),
            pl.BlockSpec((D, LANES), lambda i: (0, 0)),
            pl.BlockSpec((1, LANES), lambda i: (0, 0)),
        ],
        out_specs=[pl.BlockSpec((tr, D // 2), lambda i: (i, 0)), slab, slab, slab,
                   pl.BlockSpec((SUBLANES, LANES), lambda i: (0, 0))],
        out_shape=[jax.ShapeDtypeStruct((T, D // 2), jnp.uint32), jax.ShapeDtypeStruct((T, LANES), jnp.int32),
                   jax.ShapeDtypeStruct((T, LANES), F32), jax.ShapeDtypeStruct((T, LANES), jnp.int32),
                   jax.ShapeDtypeStruct((SUBLANES, LANES), F32)],
        scratch_shapes=[pltpu.VMEM((1, LANES), F32)],
        compiler_params=_params(("arbitrary",)),
        name="router",
    )(xs, g.reshape(1, D), mod, mod, wr, br)


def _dispatch_kernel(fill_lo_ref, fill_hi_ref, nu_ref, dest_ref, h_ref, xs_ref, zblk, sem, zsem, *, bm):
    tr = h_ref.shape[0]

    @pl.when(pl.program_id(0) == 0)
    def _():
        zblk[...] = jnp.zeros_like(zblk)
        n_blocks = xs_ref.shape[0] // bm

        def zero_row(r):
            return pltpu.make_async_copy(zblk.at[pl.ds(0, 1), :], xs_ref.at[pl.ds(r, 1), :], zsem)

        def zero_block(b):
            return pltpu.make_async_copy(zblk, xs_ref.at[pl.ds(pl.multiple_of(b * bm, bm), bm), :], zsem)

        def per_expert(e, c):
            lax.fori_loop(fill_lo_ref[e], fill_hi_ref[e], lambda r, c2: (zero_row(r).start(), c2)[1], 0)
            return c

        lax.fori_loop(0, fill_lo_ref.shape[0], per_expert, 0)
        lax.fori_loop(nu_ref[0], n_blocks, lambda b, c: (zero_block(b).start(), c)[1], 0)

        def per_expert_wait(e, c):
            lax.fori_loop(fill_lo_ref[e], fill_hi_ref[e], lambda r, c2: (zero_row(r).wait(), c2)[1], 0)
            return c

        lax.fori_loop(0, fill_lo_ref.shape[0], per_expert_wait, 0)
        lax.fori_loop(nu_ref[0], n_blocks, lambda b, c: (zero_block(b).wait(), c)[1], 0)

    def copy(r, k):
        return pltpu.make_async_copy(h_ref.at[pl.ds(r, 1), :], xs_ref.at[pl.ds(dest_ref[0, r * TOP_K + k], 1), :], sem)

    def issue(r, c):
        for k in range(TOP_K):
            copy(r, k).start()
        return c

    lax.fori_loop(0, tr, issue, 0)

    def drain(r, c):
        for k in range(TOP_K):
            copy(r, k).wait()
        return c

    lax.fori_loop(0, tr, drain, 0)


def _dispatch(h, dest, fill_lo, fill_hi, n_used, n_rows, tr, bm):
    T, D = h.shape
    grid_spec = pltpu.PrefetchScalarGridSpec(
        num_scalar_prefetch=3,
        grid=(T // tr,),
        in_specs=[
            pl.BlockSpec((None, 1, tr * TOP_K), lambda i, lo, hi, nu: (i, 0, 0), memory_space=pltpu.SMEM),
            pl.BlockSpec((tr, D), lambda i, lo, hi, nu: (i, 0)),
        ],
        out_specs=pl.BlockSpec(memory_space=pl.ANY),
        scratch_shapes=[pltpu.VMEM((bm, D), h.dtype), pltpu.SemaphoreType.DMA(()), pltpu.SemaphoreType.DMA(())],
    )
    return pl.pallas_call(
        functools.partial(_dispatch_kernel, bm=bm),
        grid_spec=grid_spec,
        out_shape=jax.ShapeDtypeStruct((n_rows, D), h.dtype),
        compiler_params=_params(("arbitrary",)),
        name="dispatch",
    )(fill_lo, fill_hi, n_used, dest.reshape(T // tr, 1, tr * TOP_K), h)


def _deinterleave(gu):
    r, width = gu.shape
    lane = lax.broadcasted_iota(jnp.int32, (r, LANES), 1)
    idx_even = (2 * lane) % LANES
    idx_odd = idx_even + 1
    first = lane < LANES // 2
    even, odd = [], []
    for c in range(width // (2 * LANES)):
        xa = gu[:, 2 * LANES * c:2 * LANES * c + LANES]
        xb = gu[:, 2 * LANES * c + LANES:2 * LANES * (c + 1)]
        even.append(jnp.where(first, jnp.take_along_axis(xa, idx_even, axis=1), jnp.take_along_axis(xb, idx_even, axis=1)))
        odd.append(jnp.where(first, jnp.take_along_axis(xa, idx_odd, axis=1), jnp.take_along_axis(xb, idx_odd, axis=1)))
    return jnp.concatenate(even, axis=1), jnp.concatenate(odd, axis=1)


CAST_ROWS = 512


def _refresh_cache(be_ref, w_ref, cache):
    i = pl.program_id(0)

    @pl.when(jnp.logical_or(i == 0, be_ref[i] != be_ref[jnp.maximum(i - 1, 0)]))
    def _():
        rows = w_ref.shape[0]
        step = min(CAST_ROWS, rows)
        for r0 in range(0, rows, step):
            cache[r0:r0 + step, :] = w_ref[r0:r0 + step, :].astype(BF16)


def _expert_up_kernel(be_ref, nu_ref, x_ref, wgu_ref, bgu_ref, o_ref, wcache):
    _refresh_cache(be_ref, wgu_ref, wcache)
    i = pl.program_id(0)

    @pl.when(i < nu_ref[0])
    def _():
        x_lo, x_hi = _unpack_rows(x_ref[...])
        x = jnp.concatenate([x_lo.astype(BF16), x_hi.astype(BF16)], axis=1)
        gu = jnp.dot(x, wcache[...], preferred_element_type=F32) + bgu_ref[...]
        glu, lin = _deinterleave(gu)
        glu = jnp.minimum(glu, SWIGLU_LIMIT)
        lin = jnp.clip(lin, -SWIGLU_LIMIT, SWIGLU_LIMIT)
        o_ref[...] = (glu * jax.nn.sigmoid(SWIGLU_ALPHA * glu) * (lin + 1.0)).astype(o_ref.dtype)

    @pl.when(i >= nu_ref[0])
    def _():
        o_ref[...] = jnp.zeros_like(o_ref)


def _expert_down_kernel(be_ref, nu_ref, a_ref, wd_ref, bd_ref, o_ref, wcache):
    _refresh_cache(be_ref, wd_ref, wcache)
    i = pl.program_id(0)

    @pl.when(i < nu_ref[0])
    def _():
        o_ref[...] = _pack_rows(jnp.dot(a_ref[...], wcache[...], preferred_element_type=F32) + bd_ref[...])

    @pl.when(i >= nu_ref[0])
    def _():
        o_ref[...] = jnp.zeros_like(o_ref)


def _experts(xs, block_e, n_used, wgu, bgu, wd, bd, l, bm):
    n_rows, Dw = xs.shape
    L, E, D, F2 = wgu.shape
    F = F2 // 2
    n_blocks = n_rows // bm

    def used(i, nu):
        return jnp.minimum(i, nu[0] - 1)

    up_spec = pltpu.PrefetchScalarGridSpec(
        num_scalar_prefetch=2,
        grid=(n_blocks,),
        in_specs=[
            pl.BlockSpec((bm, Dw), lambda i, be, nu: (used(i, nu), 0)),
            pl.BlockSpec((None, None, D, F2), lambda i, be, nu: (l, be[i], 0, 0)),
            pl.BlockSpec((None, None, 1, F2), lambda i, be, nu: (l, be[i], 0, 0)),
        ],
        out_specs=pl.BlockSpec((bm, F), lambda i, be, nu: (i, 0)),
        scratch_shapes=[pltpu.VMEM((D, F2), BF16)],
    )
    act = pl.pallas_call(
        _expert_up_kernel,
        grid_spec=up_spec,
        out_shape=jax.ShapeDtypeStruct((n_rows, F), BF16),
        compiler_params=_params(("arbitrary",)),
        name="expert_up",
    )(block_e, n_used, xs, wgu, bgu.reshape(L, E, 1, F2))
    down_spec = pltpu.PrefetchScalarGridSpec(
        num_scalar_prefetch=2,
        grid=(n_blocks,),
        in_specs=[
            pl.BlockSpec((bm, F), lambda i, be, nu: (i, 0)),
            pl.BlockSpec((None, None, F, D), lambda i, be, nu: (l, be[i], 0, 0)),
            pl.BlockSpec((None, None, 1, D), lambda i, be, nu: (l, be[i], 0, 0)),
        ],
        out_specs=pl.BlockSpec((bm, Dw), lambda i, be, nu: (i, 0)),
        scratch_shapes=[pltpu.VMEM((F, D), BF16)],
    )
    return pl.pallas_call(
        _expert_down_kernel,
        grid_spec=down_spec,
        out_shape=jax.ShapeDtypeStruct((n_rows, Dw), jnp.uint32),
        compiler_params=_params(("arbitrary",)),
        name="expert_down",
    )(block_e, n_used, act, wd, bd.reshape(L, E, 1, D))


def _combine_kernel(dest_ref, ys_ref, gatew_ref, x_ref, mod_ref, fg_ref, o_ref, buf, sem, *, m_tiles, final):
    tr = x_ref.shape[0]
    is_ctx = pl.program_id(0) < m_tiles

    def copy(r, k):
        return pltpu.make_async_copy(ys_ref.at[pl.ds(dest_ref[0, r * TOP_K + k], 1), :], buf.at[k, pl.ds(r, 1), :], sem)

    def issue(r, c):
        for k in range(TOP_K):
            copy(r, k).start()
        return c

    lax.fori_loop(0, tr, issue, 0)

    def drain(r, c):
        for k in range(TOP_K):
            copy(r, k).wait()
        return c

    lax.fori_loop(0, tr, drain, 0)

    half = x_ref.shape[1] // 2
    g = jnp.where(is_ctx, mod_ref[1:2, :], mod_ref[0:1, :])

    def piece(t, c):
        rs = pl.ds(pl.multiple_of(t * SUBLANES, SUBLANES), SUBLANES)
        gw = gatew_ref[rs, :]
        f_lo = f_hi = None
        for k in range(TOP_K):
            lo, hi = _unpack_rows(buf[k, rs, :])
            f_lo = gw[:, k:k + 1] * lo if f_lo is None else f_lo + gw[:, k:k + 1] * lo
            f_hi = gw[:, k:k + 1] * hi if f_hi is None else f_hi + gw[:, k:k + 1] * hi
        y_lo = x_ref[rs, 0:half] + g[:, 0:half] * f_lo
        y_hi = x_ref[rs, half:] + g[:, half:] * f_hi
        if final:
            ssq = jnp.sum(y_lo * y_lo, axis=-1, keepdims=True) + jnp.sum(y_hi * y_hi, axis=-1, keepdims=True)
            inv = lax.rsqrt(ssq / (2 * half) + NORM_EPS)
            y_lo = y_lo * inv * fg_ref[:, 0:half]
            y_hi = y_hi * inv * fg_ref[:, half:]
        o_ref[rs, 0:half] = y_lo
        o_ref[rs, half:] = y_hi
        return c

    lax.fori_loop(0, tr // SUBLANES, piece, 0)


def _combine(ys, dest, gatew, xs, mod, gate_idx, final_g, m, tr, final):
    T, D = xs.shape
    mt = m // tr
    out_rows = T - m if final else T
    out_map = (lambda i: (jnp.maximum(i - mt, 0), 0)) if final else (lambda i: (i, 0))
    return pl.pallas_call(
        functools.partial(_combine_kernel, m_tiles=mt, final=final),
        grid=(T // tr,),
        in_specs=[
            pl.BlockSpec((None, 1, tr * TOP_K), lambda i: (i, 0, 0), memory_space=pltpu.SMEM),
            pl.BlockSpec(memory_space=pl.ANY),
            pl.BlockSpec((tr, LANES), lambda i: (i, 0)),
            pl.BlockSpec((tr, D), lambda i: (i, 0)),
            pl.BlockSpec((SUBLANES, D), lambda i: (0, gate_idx)),
            pl.BlockSpec((1, D), lambda i: (0, 0)),
        ],
        out_specs=pl.BlockSpec((tr, D), out_map),
        out_shape=jax.ShapeDtypeStruct((out_rows, D), F32),
        scratch_shapes=[pltpu.VMEM((TOP_K, tr, D // 2), jnp.uint32), pltpu.SemaphoreType.DMA(())],
        compiler_params=_params(("arbitrary",)),
        name="combine",
    )(dest.reshape(T // tr, 1, tr * TOP_K), ys, gatew, xs, mod, final_g.reshape(1, D))


def kernel(x, c, ctx, c_ctx, w_ada, b_ada, norm1_g, norm2_g, w_in, lam_qk, subln_g, conv_w, conv_b, lru_gate_w,
           lru_gate_b, lru_lambda, na_rpb, w_branch, w_out, w_router, b_router, w_gu, b_gu, w_down, b_down, final_g):
    B, n, D = x.shape
    m = ctx.shape[1]
    assert B == 1 and n % (NA_QROWS * GRID_W) == 0
    L = w_ada.shape[0]
    E = w_router.shape[-1]
    T = m + n
    rows = n // GRID_W
    tr = 256
    assert m % tr == 0 and T % tr == 0 and m % (NA_QROWS * GRID_W) == 0
    tm = _tile(T, 1408, 128)
    tn = _tile(D, 512, LANES)
    bm = 256
    n_blocks = -(-(T * TOP_K + E * (bm - 1)) // bm)
    n_rows = n_blocks * bm

    xs = jnp.concatenate([ctx[0], x[0]], axis=0)
    mod_all = _ada(c, c_ctx, w_ada, b_ada)
    cos, sin = _rope_tables(m, n)
    w_branch_b, w_out_b = w_branch.astype(BF16), w_out.astype(BF16)

    for l in range(L):
        lambda_init = 0.8 - 0.6 * math.exp(-0.3 * l)
        mod = mod_all[l]
        h = _norm_mod(xs, norm1_g[l], mod, 0, 1, m, tr)
        z = _matmul(h, w_in, l, tm, tn, BF16)
        qk = _rope(z, cos, sin, tr)
        o_a = _attn_a(qk, z, lam_qk[l], subln_g[l], lambda_init, m, 256, 1536)
        o_b = _rglru(z, conv_w[l], conv_b[l], lru_gate_w[l], lru_gate_b[l], lru_lambda[l], m, 256)
        o_c = _na(z, na_rpb[l], m)
        y = _merge(o_a, o_b, o_c, w_branch_b, l, z, D, tm, tn)
        xs = _outproj(y, w_out_b, l, xs, mod, 2, m, tm, tn)

        h2, idx, gatew, rank, cnt = _router(xs, norm2_g[l], mod, 3, 4, w_router[l], b_router[l], m, tr)
        counts = cnt[0, :E].astype(jnp.int32)
        padded = (counts + bm - 1) // bm * bm
        pend = jnp.cumsum(padded)
        pstart = pend - padded
        dest = (pstart[idx[:, :TOP_K]] + rank[:, :TOP_K]).astype(jnp.int32)
        block_start = jnp.arange(n_blocks, dtype=jnp.int32) * bm
        block_e = jnp.minimum(jnp.sum((pend[None, :] <= block_start[:, None]).astype(jnp.int32), axis=1), E - 1)
        n_used = (pend[-1:] // bm).astype(jnp.int32)

        xg = _dispatch(h2, dest, (pstart + counts).astype(jnp.int32), pend.astype(jnp.int32), n_used, n_rows, tr, bm)
        ys = _experts(xg, block_e, n_used, w_gu, b_gu, w_down, b_down, l, bm)
        xs = _combine(ys, dest, gatew, xs, mod, 5, final_g, m, tr, final=(l == L - 1))

    return xs[None]
```

```python
import functools
import math

import numpy as np
import jax
import jax.numpy as jnp
from jax import lax
from jax.experimental import pallas as pl
from jax.experimental.pallas import tpu as pltpu

F32 = jnp.float32
BF16 = jnp.bfloat16

GRID_W = 64
NORM_EPS = 1e-6
A_HEADS, A_DH = 8, 64
ROPE_PAIRS = A_DH // 4
ROPE_BASE = 10000.0
B_WIDTH, B_BW, B_CONV, CONV_LEFT, LRU_C = 1024, 128, 4, 2, 8.0
C_HEADS, C_DH, NA_ROWS, NA_COLS = 16, 64, 8, 16
N_BRANCH, BRANCH_W = 3, 1024
TOP_K = 4
SWIGLU_ALPHA, SWIGLU_LIMIT = 1.702, 7.0

LANES = 128
SUBLANES = 8
COL_AQ, COL_AK, COL_AV, COL_BX, COL_BY, COL_CQ, COL_CK, COL_CV, COL_G = 0, 8, 16, 24, 32, 40, 48, 56, 64
NEG_BIG = -1e30
VMEM_LIMIT = 56 * 1024 * 1024


def _params(sem):
    return pltpu.CompilerParams(dimension_semantics=sem, vmem_limit_bytes=VMEM_LIMIT)


def _tile(dim, target, mult):
    best = None
    for t in range(mult, min(dim, target) + 1, mult):
        if dim % t == 0:
            best = t
    assert best is not None, (dim, target, mult)
    return best


def _ada_kernel(c_ref, w_ref, b_ref, o_ref, acc_ref):
    k = pl.program_id(2)

    @pl.when(k == 0)
    def _():
        acc_ref[...] = jnp.zeros_like(acc_ref)

    cc = c_ref[...]
    s = cc * jax.nn.sigmoid(cc)
    hi = s.astype(BF16)
    lo = (s - hi.astype(F32)).astype(BF16)
    row = lax.broadcasted_iota(jnp.int32, s.shape, 0)
    lhs = jnp.where(row < 2, hi, lo)
    acc_ref[...] += jnp.dot(lhs, w_ref[...].astype(BF16), preferred_element_type=F32)

    @pl.when(k == pl.num_programs(2) - 1)
    def _():
        acc = acc_ref[...]
        o_ref[...] = acc + pltpu.roll(acc, SUBLANES - 2, 0) + b_ref[...]


def _ada(c, c_ctx, w_ada, b_ada):
    L, D, N = w_ada.shape
    cc = jnp.concatenate([c[:1], c_ctx[None], c[:1], c_ctx[None], jnp.zeros((4, D), F32)], axis=0)
    tk = _tile(D, 1024, LANES)
    tn = _tile(N, 2048, LANES)
    return pl.pallas_call(
        _ada_kernel,
        grid=(L, N // tn, D // tk),
        in_specs=[
            pl.BlockSpec((SUBLANES, tk), lambda l, j, k: (0, k)),
            pl.BlockSpec((None, tk, tn), lambda l, j, k: (l, k, j)),
            pl.BlockSpec((None, 1, tn), lambda l, j, k: (l, 0, j)),
        ],
        out_specs=pl.BlockSpec((None, SUBLANES, tn), lambda l, j, k: (l, 0, j)),
        out_shape=jax.ShapeDtypeStruct((L, SUBLANES, N), F32),
        scratch_shapes=[pltpu.VMEM((SUBLANES, tn), F32)],
        compiler_params=_params(("parallel", "parallel", "arbitrary")),
        name="ada",
    )(cc, w_ada, b_ada.reshape(L, 1, N))


def _norm_mod_kernel(x_ref, g_ref, sh_ref, sc_ref, o_ref, *, m_tiles):
    is_ctx = pl.program_id(0) < m_tiles
    x = x_ref[...]
    y = x * lax.rsqrt(jnp.mean(x * x, axis=-1, keepdims=True) + NORM_EPS) * g_ref[...]
    sh = jnp.where(is_ctx, sh_ref[1:2, :], sh_ref[0:1, :])
    sc = jnp.where(is_ctx, sc_ref[1:2, :], sc_ref[0:1, :])
    o_ref[...] = (y * (1.0 + sc) + sh).astype(o_ref.dtype)


def _norm_mod(xs, g, mod, shift_idx, scale_idx, m, tr):
    T, D = xs.shape
    return pl.pallas_call(
        functools.partial(_norm_mod_kernel, m_tiles=m // tr),
        grid=(T // tr,),
        in_specs=[
            pl.BlockSpec((tr, D), lambda i: (i, 0)),
            pl.BlockSpec((1, D), lambda i: (0, 0)),
            pl.BlockSpec((SUBLANES, D), lambda i: (0, shift_idx)),
            pl.BlockSpec((SUBLANES, D), lambda i: (0, scale_idx)),
        ],
        out_specs=pl.BlockSpec((tr, D), lambda i: (i, 0)),
        out_shape=jax.ShapeDtypeStruct((T, D), BF16),
        compiler_params=_params(("parallel",)),
        name="norm_mod",
    )(xs, g.reshape(1, D), mod, mod)


def _mm_kernel(a_ref, b_ref, o_ref):
    o_ref[...] = jnp.dot(a_ref[...], b_ref[...].astype(BF16), preferred_element_type=F32).astype(o_ref.dtype)


def _matmul(a, b_all, l, tm, tn, out_dtype):
    M, K = a.shape
    N = b_all.shape[-1]
    return pl.pallas_call(
        _mm_kernel,
        grid=(M // tm, N // tn),
        in_specs=[pl.BlockSpec((tm, K), lambda i, j: (i, 0)), pl.BlockSpec((None, K, tn), lambda i, j: (l, 0, j))],
        out_specs=pl.BlockSpec((tm, tn), lambda i, j: (i, j)),
        out_shape=jax.ShapeDtypeStruct((M, N), out_dtype),
        compiler_params=_params(("parallel", "arbitrary")),
        name="in_proj",
    )(a, b_all)


def _rope_tables(m, n):
    rows = n // GRID_W
    inv = jnp.asarray(ROPE_BASE, F32) ** (-jnp.arange(ROPE_PAIRS, dtype=F32) / ROPE_PAIRS)
    ang_r = jnp.arange(rows, dtype=F32)[:, None] * inv
    ang_c = jnp.arange(GRID_W, dtype=F32)[:, None] * inv

    def table(fn, sign_lo):
        tr_ = jnp.broadcast_to(fn(ang_r)[:, None, :], (rows, GRID_W, ROPE_PAIRS))
        tc_ = jnp.broadcast_to(fn(ang_c)[None, :, :], (rows, GRID_W, ROPE_PAIRS))
        vec = jnp.concatenate([sign_lo * tr_, tr_, sign_lo * tc_, tc_], axis=-1)
        full = jnp.concatenate([vec] * (LANES // A_DH), axis=-1).reshape(n, LANES)
        return full

    cos = jnp.concatenate([jnp.ones((m, LANES), F32), table(jnp.cos, 1.0)], axis=0)
    sin = jnp.concatenate([jnp.zeros((m, LANES), F32), table(jnp.sin, -1.0)], axis=0)
    return cos, sin


def _rope_kernel(z_ref, cos_ref, sin_ref, o_ref):
    scale = jnp.where(pl.program_id(1) == 0, A_DH ** -0.5, 1.0).astype(F32)
    cos = cos_ref[...]
    sin = sin_ref[...]
    lane = lax.broadcasted_iota(jnp.int32, cos.shape, 1)
    lo = (lane % (2 * ROPE_PAIRS)) < ROPE_PAIRS
    for hc in range(z_ref.shape[1] // LANES):
        x = z_ref[:, hc * LANES:(hc + 1) * LANES].astype(F32)
        partner = jnp.where(lo, pltpu.roll(x, LANES - ROPE_PAIRS, 1), pltpu.roll(x, ROPE_PAIRS, 1))
        o_ref[:, hc * LANES:(hc + 1) * LANES] = ((x * cos + partner * sin) * scale).astype(o_ref.dtype)


def _rope(z, cos, sin, tr):
    T = z.shape[0]
    W = A_HEADS * 2 * A_DH
    return pl.pallas_call(
        _rope_kernel,
        grid=(T // tr, 2),
        in_specs=[
            pl.BlockSpec((tr, W), lambda i, j: (i, j)),
            pl.BlockSpec((tr, LANES), lambda i, j: (i, 0)),
            pl.BlockSpec((tr, LANES), lambda i, j: (i, 0)),
        ],
        out_specs=pl.BlockSpec((tr, W), lambda i, j: (i, j)),
        out_shape=jax.ShapeDtypeStruct((T, 2 * W), BF16),
        compiler_params=_params(("parallel", "arbitrary")),
        name="rope",
    )(z, cos, sin)


def _attn_a_kernel(lam_ref, g_ref, q_ref, k_ref, v_ref, o_ref, *, m, T, tq, tk, lambda_init):
    lq = lam_ref[...]
    lam = (jnp.exp(jnp.sum(lq[0:1] * lq[1:2], axis=-1, keepdims=True))
           - jnp.exp(jnp.sum(lq[2:3] * lq[3:4], axis=-1, keepdims=True)) + lambda_init)

    def attend(row0, nq, bounds):
        qb = q_ref[pl.ds(row0, nq), :]
        lane = lax.broadcasted_iota(jnp.int32, qb.shape, 1)
        zero = jnp.zeros_like(qb)
        qs = jnp.concatenate([jnp.where(lane < A_DH, qb, zero), jnp.where(lane >= A_DH, qb, zero)], axis=0)

        def scores(lo, hi):
            return lax.dot_general(qs, k_ref[lo:hi, :], (((1,), (1,)), ((), ())), preferred_element_type=F32)

        carry = (jnp.full((2 * nq, 1), NEG_BIG, F32), jnp.zeros((2 * nq, 1), F32), jnp.zeros((2 * nq, LANES), F32))
        s = scores(*bounds[0])
        for c, (lo, hi) in enumerate(bounds):
            s_next = scores(*bounds[c + 1]) if c + 1 < len(bounds) else None
            m_i, l_i, acc = carry
            m_new = jnp.maximum(m_i, jnp.max(s, axis=-1, keepdims=True))
            alpha = jnp.exp(m_i - m_new)
            p = jnp.exp(s - m_new)
            l_new = alpha * l_i + jnp.sum(p, axis=-1, keepdims=True)
            acc = alpha * acc + jnp.dot(p.astype(BF16), v_ref[lo:hi, :], preferred_element_type=F32)
            carry = (m_new, l_new, acc)
            s = s_next
        _, l_i, acc = carry
        o = acc / l_i
        d = o[:nq] - lam * o[nq:]
        y = d * lax.rsqrt(jnp.mean(d * d, axis=-1, keepdims=True) + NORM_EPS) * g_ref[...]
        o_ref[pl.ds(row0, nq), :] = (y * (1.0 - lambda_init)).astype(o_ref.dtype)

    @pl.when(pl.program_id(1) == 0)
    def _():
        attend(0, m, [(0, m)])

    attend(pl.multiple_of(m + pl.program_id(1) * tq, math.gcd(m, tq)), tq,
           [(lo, min(lo + tk, T)) for lo in range(0, T, tk)])


def _attn_a(qk, z, lam_qk, subln_g, lambda_init, m, tq, tk):
    T = z.shape[0]
    assert (T - m) % tq == 0
    col = lambda base: pl.BlockSpec((T, LANES), lambda h, i: (0, base + h))
    return pl.pallas_call(
        functools.partial(_attn_a_kernel, m=m, T=T, tq=tq, tk=tk, lambda_init=lambda_init),
        grid=(A_HEADS, (T - m) // tq),
        in_specs=[
            pl.BlockSpec((4, A_DH), lambda h, i: (0, 0)),
            pl.BlockSpec((1, 2 * A_DH), lambda h, i: (0, 0)),
            col(0), col(A_HEADS), col(COL_AV),
        ],
        out_specs=pl.BlockSpec((T, LANES), lambda h, i: (0, h)),
        out_shape=jax.ShapeDtypeStruct((T, A_HEADS * 2 * A_DH), BF16),
        compiler_params=_params(("parallel", "arbitrary")),
        name="diff_attn",
    )(lam_qk, subln_g.reshape(1, 2 * A_DH), qk, qk, z)


def _rglru_kernel(bx_ref, by_ref, cw_ref, cb_ref, gw_ref, gb_ref, lam_ref, o_ref, xpad, hf, *, m, T, tc):
    nchunk = T // tc
    mc = m // tc
    nb = tc // SUBLANES
    halo = SUBLANES
    xpad[0:halo, :] = jnp.zeros((halo, LANES), F32)
    xpad[halo + T:2 * halo + T, :] = jnp.zeros((halo, LANES), F32)
    xpad[halo:halo + T, :] = bx_ref[...].astype(F32)

    row = lax.broadcasted_iota(jnp.int32, (tc, LANES), 0)
    sub = lax.broadcasted_iota(jnp.int32, (nb, SUBLANES, LANES), 1)
    cw = cw_ref[...]
    cb = cb_ref[...]

    def coeffs(j, d):
        r0 = j * tc
        left_cut = r0 == m
        right_cut = (r0 + tc) == m
        x_m2 = xpad[pl.ds(r0 + halo - 2, tc), :]
        x_m1 = xpad[pl.ds(r0 + halo - 1, tc), :]
        x_0 = xpad[pl.ds(r0 + halo, tc), :]
        x_p1 = xpad[pl.ds(r0 + halo + 1, tc), :]
        x_m2 = jnp.where(jnp.logical_and(left_cut, row < 2), 0.0, x_m2)
        x_m1 = jnp.where(jnp.logical_and(left_cut, row < 1), 0.0, x_m1)
        x_p1 = jnp.where(jnp.logical_and(right_cut, row >= tc - 1), 0.0, x_p1)
        u = cb + x_m2 * cw[0:1] + x_m1 * cw[1:2] + x_0 * cw[2:3] + x_p1 * cw[3:4]
        ub = u.astype(BF16)
        gr = jnp.dot(ub, gw_ref[d, 0].astype(BF16), preferred_element_type=F32) + gb_ref[d, 0]
        gi = jnp.dot(ub, gw_ref[d, 1].astype(BF16), preferred_element_type=F32) + gb_ref[d, 1]
        r = jax.nn.sigmoid(gr)
        i = jax.nn.sigmoid(gi)
        log_a = (-LRU_C * jax.nn.softplus(-lam_ref[d])) * r
        a = jnp.exp(log_a)
        b = jnp.sqrt(-jnp.tanh(log_a) * (a * a + 1.0)) * (i * u)
        return a, b

    def chunk_scan(a, b, h, rev):
        A = a.reshape(nb, SUBLANES, LANES)
        B = b.reshape(nb, SUBLANES, LANES)
        for s in (1, 2, 4):
            shift = SUBLANES - s if rev else s
            msk = (sub < SUBLANES - s) if rev else (sub >= s)
            Ap = pltpu.roll(A, shift, 1)
            Bp = pltpu.roll(B, shift, 1)
            B = jnp.where(msk, A * Bp + B, B)
            A = jnp.where(msk, A * Ap, A)
        outs = [None] * nb
        order = range(nb - 1, -1, -1) if rev else range(nb)
        for jb in order:
            hb = A[jb] * h + B[jb]
            outs[jb] = hb
            edge = hb[0:1, :] if rev else hb[SUBLANES - 1:SUBLANES, :]
            h = jnp.broadcast_to(edge, (SUBLANES, LANES))
        return jnp.concatenate(outs, axis=0), h

    def fwd_body(j, h):
        a, b = coeffs(j, 0)
        hc, h = chunk_scan(a, b, h, False)
        hf[pl.ds(pl.multiple_of(j * tc, tc), tc), :] = hc
        return h

    lax.fori_loop(0, nchunk, fwd_body, jnp.zeros((SUBLANES, LANES), F32))

    def bwd_body(s, h):
        j = jnp.where(s < mc, mc - 1 - s, nchunk - 1 - (s - mc))
        a, b = coeffs(j, 1)
        hc, h = chunk_scan(a, b, h, True)
        r0 = pl.multiple_of(j * tc, tc)
        by = by_ref[pl.ds(r0, tc), :].astype(F32)
        o_ref[pl.ds(r0, tc), :] = ((hf[pl.ds(r0, tc), :] + hc) * jax.nn.gelu(by)).astype(o_ref.dtype)
        return h

    lax.fori_loop(0, nchunk, bwd_body, jnp.zeros((SUBLANES, LANES), F32))


def _rglru(z, conv_w, conv_b, gate_w, gate_b, lru_lambda, m, tc):
    T = z.shape[0]
    nblk = B_WIDTH // B_BW
    return pl.pallas_call(
        functools.partial(_rglru_kernel, m=m, T=T, tc=tc),
        grid=(nblk,),
        in_specs=[
            pl.BlockSpec((T, LANES), lambda c: (0, COL_BX + c)),
            pl.BlockSpec((T, LANES), lambda c: (0, COL_BY + c)),
            pl.BlockSpec((B_CONV, LANES), lambda c: (0, c)),
            pl.BlockSpec((1, LANES), lambda c: (0, c)),
            pl.BlockSpec((2, 2, None, B_BW, B_BW), lambda c: (0, 0, c, 0, 0)),
            pl.BlockSpec((2, 2, None, 1, B_BW), lambda c: (0, 0, c, 0, 0)),
            pl.BlockSpec((2, 1, LANES), lambda c: (0, 0, c)),
        ],
        out_specs=pl.BlockSpec((T, LANES), lambda c: (0, c)),
        out_shape=jax.ShapeDtypeStruct((T, B_WIDTH), BF16),
        scratch_shapes=[pltpu.VMEM((T + 2 * SUBLANES, LANES), F32), pltpu.VMEM((T, LANES), F32)],
        compiler_params=_params(("parallel",)),
        name="rglru",
    )(z, z, conv_w, conv_b.reshape(1, B_WIDTH), gate_w, gate_b.reshape(2, 2, nblk, 1, B_BW),
      lru_lambda.reshape(2, 1, B_WIDTH))


NA_QROWS = 4
NA_KROWS = NA_QROWS + NA_ROWS - 1
NA_UNROLL = 4


def _na_base(g, rows):
    return jnp.clip(NA_QROWS * g - NA_ROWS // 2, 0, rows - NA_KROWS)


def _na_patterns(rows):
    sigs, reps, pat_of_g = {}, [], []
    for g in range(rows // NA_QROWS):
        base = int(np.clip(NA_QROWS * g - NA_ROWS // 2, 0, rows - NA_KROWS))
        sig = (base - NA_QROWS * g,) + tuple(
            int(np.clip(r - NA_ROWS // 2, 0, rows - NA_ROWS)) - r for r in range(NA_QROWS * g, NA_QROWS * (g + 1)))
        if sig not in sigs:
            sigs[sig] = len(reps)
            reps.append(g)
        pat_of_g.append(sigs[sig])
    return reps, pat_of_g


def _na_row_rel(rows, reps):
    rel = []
    for g in reps:
        base = int(np.clip(NA_QROWS * g - NA_ROWS // 2, 0, rows - NA_KROWS))
        per_rho = []
        for rho in range(NA_QROWS):
            r = NA_QROWS * g + rho
            r0 = int(np.clip(r - NA_ROWS // 2, 0, rows - NA_ROWS))
            per_rho.append([base + j - r + NA_ROWS - 1 if r0 <= base + j < r0 + NA_ROWS else None
                            for j in range(NA_KROWS)])
        rel.append(per_rho)
    return rel


def _na_col_selectors():
    n_cr = 2 * NA_COLS - 1
    sel = np.zeros((2, n_cr, GRID_W, LANES), np.float32)
    mask = np.zeros((2, GRID_W, LANES), np.float32)
    neg = np.zeros((2, GRID_W, LANES), np.float32)
    for half in range(2):
        neg[half, :, half * GRID_W:(half + 1) * GRID_W] = NEG_BIG
        mask[half, :, half * GRID_W:(half + 1) * GRID_W] = NEG_BIG
        for c in range(GRID_W):
            c0 = int(np.clip(c - NA_COLS // 2, 0, GRID_W - NA_COLS))
            for kc in range(c0, c0 + NA_COLS):
                sel[half, kc - c + NA_COLS - 1, c, half * GRID_W + kc] = 1.0
                mask[half, c, half * GRID_W + kc] = 0.0
    return sel, mask, neg


def _na_bias_kernel(rpb_ref, sel_ref, mask_ref, neg_ref, o_ref, blocks, *, rel):
    h = pl.program_id(0)
    n_rr, n_cr = 2 * NA_ROWS - 1, 2 * NA_COLS - 1

    def build(r, carry):
        for half in range(2):
            acc = mask_ref[half]
            for col in range(n_cr):
                acc = acc + rpb_ref[h, r, col] * sel_ref[half, col]
            blocks[half, r] = acc
        return carry

    lax.fori_loop(0, n_rr, build, 0)
    for p, per_rho in enumerate(rel):
        for rho, per_j in enumerate(per_rho):
            rs = slice(rho * GRID_W, (rho + 1) * GRID_W)
            for jj in range(0, NA_KROWS, 2):
                lo = blocks[0, per_j[jj]] if per_j[jj] is not None else neg_ref[0]
                if jj + 1 < NA_KROWS:
                    hi = blocks[1, per_j[jj + 1]] if per_j[jj + 1] is not None else neg_ref[1]
                    o_ref[p, rs, jj * GRID_W:(jj + 2) * GRID_W] = lo + hi
                else:
                    o_ref[p, rs, jj * GRID_W:(jj + 1) * GRID_W] = lo[:, 0:GRID_W]


def _na_bias(rpb, rows, reps):
    H = rpb.shape[0]
    n_rr = 2 * NA_ROWS - 1
    sel, mask, neg = _na_col_selectors()
    tq, nk = NA_QROWS * GRID_W, NA_KROWS * GRID_W
    whole = lambda a: pl.BlockSpec(a.shape, lambda h: (0,) * a.ndim)
    return pl.pallas_call(
        functools.partial(_na_bias_kernel, rel=_na_row_rel(rows, reps)),
        grid=(H,),
        in_specs=[pl.BlockSpec(memory_space=pltpu.SMEM), whole(sel), whole(mask), whole(neg)],
        out_specs=pl.BlockSpec((len(reps), None, tq, nk), lambda h: (0, h, 0, 0)),
        out_shape=jax.ShapeDtypeStruct((len(reps), H, tq, nk), F32),
        scratch_shapes=[pltpu.VMEM((2, n_rr, GRID_W, LANES), F32)],
        compiler_params=_params(("parallel",)),
        name="nbr_bias",
    )(rpb.astype(F32), jnp.asarray(sel), jnp.asarray(mask), jnp.asarray(neg))


def _na_kernel(pat_ref, q_ref, k_ref, v_ref, bias_ref, o_ref, *, m, rows):
    tq = NA_QROWS * GRID_W
    nk = NA_KROWS * GRID_W
    lane = lax.broadcasted_iota(jnp.int32, (tq, LANES), 1)
    nt = (((1,), (1,)), ((), ()))

    def stacked_q(row0):
        qb = q_ref[pl.ds(row0, tq), :] * jnp.asarray(C_DH ** -0.5, BF16)
        zero = jnp.zeros_like(qb)
        return jnp.concatenate([jnp.where(lane < C_DH, qb, zero), jnp.where(lane >= C_DH, qb, zero)], axis=0)

    def emit(row0, o):
        o_ref[pl.ds(row0, tq), :] = jnp.where(lane < C_DH, o[:tq], o[tq:]).astype(o_ref.dtype)

    def softmax_pv(parts):
        mx = None
        for s, _ in parts:
            pm = jnp.max(s, axis=-1, keepdims=True)
            mx = pm if mx is None else jnp.maximum(mx, pm)
        l = None
        o = None
        for s, v in parts:
            p = jnp.exp(s - mx)
            ps = jnp.sum(p, axis=-1, keepdims=True)
            po = jnp.dot(p.astype(BF16), v, preferred_element_type=F32)
            l = ps if l is None else l + ps
            o = po if o is None else o + po
        return o / l

    for c in range(m // tq):
        emit(c * tq, softmax_pv([(lax.dot_general(stacked_q(c * tq), k_ref[0:m, :], nt, preferred_element_type=F32),
                                  v_ref[0:m, :])]))

    def group(g, carry):
        row0 = pl.multiple_of(m + g * tq, tq)
        off = pl.multiple_of(m + _na_base(g, rows) * GRID_W, GRID_W)
        qs = stacked_q(row0)
        bias = bias_ref[pat_ref[g]].reshape(2 * tq, nk)
        s_nb = lax.dot_general(qs, k_ref[pl.ds(off, nk), :], nt, preferred_element_type=F32) + bias
        s_cx = lax.dot_general(qs, k_ref[0:m, :], nt, preferred_element_type=F32)
        emit(row0, softmax_pv([(s_nb, v_ref[pl.ds(off, nk), :]), (s_cx, v_ref[0:m, :])]))
        return carry

    lax.fori_loop(0, rows // NA_QROWS, group, 0, unroll=NA_UNROLL)


def _na(z, rpb, m):
    T = z.shape[0]
    rows = (T - m) // GRID_W
    tq = NA_QROWS * GRID_W
    assert m % tq == 0 and (rows // NA_QROWS) % NA_UNROLL == 0
    reps, pat_of_g = _na_patterns(rows)
    bias = _na_bias(rpb, rows, reps)
    pat = jnp.asarray(pat_of_g, jnp.int32)
    col = lambda base: pl.BlockSpec((T, LANES), lambda h, pat: (0, base + h))
    grid_spec = pltpu.PrefetchScalarGridSpec(
        num_scalar_prefetch=1,
        grid=(C_HEADS // 2,),
        in_specs=[col(COL_CQ), col(COL_CK), col(COL_CV),
                  pl.BlockSpec((len(reps), 2, tq, NA_KROWS * GRID_W), lambda h, pat: (0, h, 0, 0))],
        out_specs=pl.BlockSpec((T, LANES), lambda h, pat: (0, h)),
    )
    return pl.pallas_call(
        functools.partial(_na_kernel, m=m, rows=rows),
        grid_spec=grid_spec,
        out_shape=jax.ShapeDtypeStruct((T, C_HEADS * C_DH), BF16),
        compiler_params=_params(("parallel",)),
        name="nbr_attn",
    )(pat, z, z, z, bias)


def _merge_kernel(oa_ref, ob_ref, oc_ref, wb_ref, g0_ref, g1_ref, g2_ref, o_ref):
    acc = None
    for i, (o, g) in enumerate(((oa_ref, g0_ref), (ob_ref, g1_ref), (oc_ref, g2_ref))):
        t = jax.nn.sigmoid(g[...].astype(F32)) * jnp.dot(o[...], wb_ref[i].astype(BF16), preferred_element_type=F32)
        acc = t if acc is None else acc + t
    o_ref[...] = acc.astype(o_ref.dtype)


def _merge(oa, ob, oc, wb_all, l, z, D, tm, tn):
    T = oa.shape[0]
    gcol = COL_G * LANES // tn
    nd = D // tn
    o_spec = pl.BlockSpec((tm, BRANCH_W), lambda i, j: (i, 0))
    return pl.pallas_call(
        _merge_kernel,
        grid=(T // tm, D // tn),
        in_specs=[o_spec, o_spec, o_spec,
                  pl.BlockSpec((None, N_BRANCH, BRANCH_W, tn), lambda i, j: (l, 0, 0, j)),
                  pl.BlockSpec((tm, tn), lambda i, j: (i, gcol + j)),
                  pl.BlockSpec((tm, tn), lambda i, j: (i, gcol + nd + j)),
                  pl.BlockSpec((tm, tn), lambda i, j: (i, gcol + 2 * nd + j))],
        out_specs=pl.BlockSpec((tm, tn), lambda i, j: (i, j)),
        out_shape=jax.ShapeDtypeStruct((T, D), BF16),
        compiler_params=_params(("parallel", "arbitrary")),
        name="merge",
    )(oa, ob, oc, wb_all, z, z, z)


def _outproj_kernel(y_ref, w_ref, x_ref, gate_ref, o_ref, *, m):
    tm = y_ref.shape[0]
    r = jnp.dot(y_ref[...], w_ref[...].astype(BF16), preferred_element_type=F32)
    row = pl.program_id(0) * tm + lax.broadcasted_iota(jnp.int32, r.shape, 0)
    g = jnp.where(row < m, gate_ref[1:2, :], gate_ref[0:1, :])
    o_ref[...] = x_ref[...] + g * r


def _outproj(y, w_all, l, xs, mod, gate_idx, m, tm, tn):
    T, D = xs.shape
    nd = D // tn
    return pl.pallas_call(
        functools.partial(_outproj_kernel, m=m),
        grid=(T // tm, nd),
        in_specs=[pl.BlockSpec((tm, D), lambda i, j: (i, 0)),
                  pl.BlockSpec((None, D, tn), lambda i, j: (l, 0, j)),
                  pl.BlockSpec((tm, tn), lambda i, j: (i, j)),
                  pl.BlockSpec((SUBLANES, tn), lambda i, j: (0, gate_idx * nd + j))],
        out_specs=pl.BlockSpec((tm, tn), lambda i, j: (i, j)),
        out_shape=jax.ShapeDtypeStruct((T, D), F32),
        compiler_params=_params(("parallel", "arbitrary")),
        name="out_proj",
    )(y, w_all, xs, mod)


HI_MASK = 0xFFFF0000


def _pack_rows(v):
    half = v.shape[1] // 2
    u = lax.bitcast_convert_type(v.astype(BF16).astype(F32), jnp.uint32)
    return (u[:, half:] & jnp.uint32(HI_MASK)) | (u[:, :half] >> 16)


def _unpack_rows(w):
    lo = lax.bitcast_convert_type(w << 16, F32)
    hi = lax.bitcast_convert_type(w & jnp.uint32(HI_MASK), F32)
    return lo, hi

def _router_kernel(x_ref, g_ref, sh_ref, sc_ref, wr_ref, br_ref, h_ref, idx_ref, gate_ref, rank_ref, cnt_ref,
                   run_ref, *, m_tiles, n_experts):
    i = pl.program_id(0)
    is_ctx = i < m_tiles

    @pl.when(i == 0)
    def _():
        run_ref[...] = jnp.zeros_like(run_ref)

    x = x_ref[...]
    tr = x.shape[0]
    y = x * lax.rsqrt(jnp.mean(x * x, axis=-1, keepdims=True) + NORM_EPS) * g_ref[...]
    sh = jnp.where(is_ctx, sh_ref[1:2, :], sh_ref[0:1, :])
    sc = jnp.where(is_ctx, sc_ref[1:2, :], sc_ref[0:1, :])
    h = y * (1.0 + sc) + sh
    hb = h.astype(BF16)
    h_ref[...] = _pack_rows(h)

    h_lo = (h - hb.astype(F32)).astype(BF16)
    w = wr_ref[...]
    w_hi = w.astype(BF16)
    w_lo = (w - w_hi.astype(F32)).astype(BF16)
    logits = (jnp.dot(hb, w_hi, preferred_element_type=F32) + jnp.dot(h_lo, w_hi, preferred_element_type=F32)
              + jnp.dot(hb, w_lo, preferred_element_type=F32)) + br_ref[...]
    lane = lax.broadcasted_iota(jnp.int32, logits.shape, 1)
    work = jnp.where(lane < n_experts, logits, NEG_BIG)

    vals, idxs = [], []
    for _ in range(TOP_K):
        mx = jnp.max(work, axis=-1, keepdims=True)
        ix = jnp.min(jnp.where(work == mx, lane, LANES), axis=-1, keepdims=True)
        vals.append(mx)
        idxs.append(ix)
        work = jnp.where(lane == ix, NEG_BIG, work)
    es = [jnp.exp(v - vals[0]) for v in vals]
    den = es[0] + es[1] + es[2] + es[3]

    onehot = jnp.zeros(logits.shape, F32)
    for ix in idxs:
        onehot = onehot + jnp.where(lane == ix, 1.0, 0.0)
    r_i = lax.broadcasted_iota(jnp.int32, (tr, tr), 0)
    c_i = lax.broadcasted_iota(jnp.int32, (tr, tr), 1)
    tri = jnp.where(c_i < r_i, 1.0, 0.0).astype(BF16)
    before = jnp.dot(tri, onehot.astype(BF16), preferred_element_type=F32) + run_ref[...]

    idx_o = jnp.zeros(logits.shape, jnp.int32)
    gate_o = jnp.zeros(logits.shape, F32)
    rank_o = jnp.zeros(logits.shape, jnp.int32)
    for k in range(TOP_K):
        rk = jnp.sum(jnp.where(lane == idxs[k], before, 0.0), axis=-1, keepdims=True)
        idx_o = jnp.where(lane == k, idxs[k], idx_o)
        gate_o = jnp.where(lane == k, es[k] / den, gate_o)
        rank_o = jnp.where(lane == k, rk.astype(jnp.int32), rank_o)
    idx_ref[...] = idx_o
    gate_ref[...] = gate_o
    rank_ref[...] = rank_o
    run_ref[...] += jnp.sum(onehot, axis=0, keepdims=True)
    cnt_ref[...] = jnp.broadcast_to(run_ref[...], cnt_ref.shape)


def _router(xs, g, mod, shift_idx, scale_idx, w_router, b_router, m, tr):
    T, D = xs.shape
    E = w_router.shape[1]
    wr = jnp.pad(w_router, ((0, 0), (0, LANES - E)))
    br = jnp.pad(b_router, (0, LANES - E)).reshape(1, LANES)
    slab = pl.BlockSpec((tr, LANES), lambda i: (i, 0))
    return pl.pallas_call(
        functools.partial(_router_kernel, m_tiles=m // tr, n_experts=E),
        grid=(T // tr,),
        in_specs=[
            pl.BlockSpec((tr, D), lambda i: (i, 0)),
            pl.BlockSpec((1, D), lambda i: (0, 0)),
            pl.BlockSpec((SUBLANES, D), lambda i: (0, shift_idx)),
            pl.BlockSpec((SUBLANES, D), lambda i: (0, scale_idx)),
            pl.BlockSpec((D, LANES), lambda i: (0, 0)),
            pl.BlockSpec((1, LANES), lambda i: (0, 0)),
        ],
        out_specs=[pl.BlockSpec((tr, D // 2), lambda i: (i, 0)), slab, slab, slab,
                   pl.BlockSpec((SUBLANES, LANES), lambda i: (0, 0))],
        out_shape=[jax.ShapeDtypeStruct((T, D // 2), jnp.uint32), jax.ShapeDtypeStruct((T, LANES), jnp.int32),
                   jax.ShapeDtypeStruct((T, LANES), F32), jax.ShapeDtypeStruct((T, LANES), jnp.int32),
                   jax.ShapeDtypeStruct((SUBLANES, LANES), F32)],
        scratch_shapes=[pltpu.VMEM((1, LANES), F32)],
        compiler_params=_params(("arbitrary",)),
        name="router",
    )(xs, g.reshape(1, D), mod, mod, wr, br)


def _dispatch_kernel(fill_lo_ref, fill_hi_ref, nu_ref, dest_ref, h_ref, xs_ref, zblk, sem, zsem, *, bm):
    tr = h_ref.shape[0]

    @pl.when(pl.program_id(0) == 0)
    def _():
        zblk[...] = jnp.zeros_like(zblk)
        n_blocks = xs_ref.shape[0] // bm

        def zero_row(r):
            return pltpu.make_async_copy(zblk.at[pl.ds(0, 1), :], xs_ref.at[pl.ds(r, 1), :], zsem)

        def zero_block(b):
            return pltpu.make_async_copy(zblk, xs_ref.at[pl.ds(pl.multiple_of(b * bm, bm), bm), :], zsem)

        def per_expert(e, c):
            lax.fori_loop(fill_lo_ref[e], fill_hi_ref[e], lambda r, c2: (zero_row(r).start(), c2)[1], 0)
            return c

        lax.fori_loop(0, fill_lo_ref.shape[0], per_expert, 0)
        lax.fori_loop(nu_ref[0], n_blocks, lambda b, c: (zero_block(b).start(), c)[1], 0)

        def per_expert_wait(e, c):
            lax.fori_loop(fill_lo_ref[e], fill_hi_ref[e], lambda r, c2: (zero_row(r).wait(), c2)[1], 0)
            return c

        lax.fori_loop(0, fill_lo_ref.shape[0], per_expert_wait, 0)
        lax.fori_loop(nu_ref[0], n_blocks, lambda b, c: (zero_block(b).wait(), c)[1], 0)

    def copy(r, k):
        return pltpu.make_async_copy(h_ref.at[pl.ds(r, 1), :], xs_ref.at[pl.ds(dest_ref[0, r * TOP_K + k], 1), :], sem)

    def issue(r, c):
        for k in range(TOP_K):
            copy(r, k).start()
        return c

    lax.fori_loop(0, tr, issue, 0)

    def drain(r, c):
        for k in range(TOP_K):
            copy(r, k).wait()
        return c

    lax.fori_loop(0, tr, drain, 0)


def _dispatch(h, dest, fill_lo, fill_hi, n_used, n_rows, tr, bm):
    T, D = h.shape
    grid_spec = pltpu.PrefetchScalarGridSpec(
        num_scalar_prefetch=3,
        grid=(T // tr,),
        in_specs=[
            pl.BlockSpec((None, 1, tr * TOP_K), lambda i, lo, hi, nu: (i, 0, 0), memory_space=pltpu.SMEM),
            pl.BlockSpec((tr, D), lambda i, lo, hi, nu: (i, 0)),
        ],
        out_specs=pl.BlockSpec(memory_space=pl.ANY),
        scratch_shapes=[pltpu.VMEM((bm, D), h.dtype), pltpu.SemaphoreType.DMA(()), pltpu.SemaphoreType.DMA(())],
    )
    return pl.pallas_call(
        functools.partial(_dispatch_kernel, bm=bm),
        grid_spec=grid_spec,
        out_shape=jax.ShapeDtypeStruct((n_rows, D), h.dtype),
        compiler_params=_params(("arbitrary",)),
        name="dispatch",
    )(fill_lo, fill_hi, n_used, dest.reshape(T // tr, 1, tr * TOP_K), h)


def _deinterleave(gu):
    r, width = gu.shape
    lane = lax.broadcasted_iota(jnp.int32, (r, LANES), 1)
    idx_even = (2 * lane) % LANES
    idx_odd = idx_even + 1
    first = lane < LANES // 2
    even, odd = [], []
    for c in range(width // (2 * LANES)):
        xa = gu[:, 2 * LANES * c:2 * LANES * c + LANES]
        xb = gu[:, 2 * LANES * c + LANES:2 * LANES * (c + 1)]
        even.append(jnp.where(first, jnp.take_along_axis(xa, idx_even, axis=1), jnp.take_along_axis(xb, idx_even, axis=1)))
        odd.append(jnp.where(first, jnp.take_along_axis(xa, idx_odd, axis=1), jnp.take_along_axis(xb, idx_odd, axis=1)))
    return jnp.concatenate(even, axis=1), jnp.concatenate(odd, axis=1)


CAST_ROWS = 512


def _refresh_cache(be_ref, w_ref, cache):
    i = pl.program_id(0)

    @pl.when(jnp.logical_or(i == 0, be_ref[i] != be_ref[jnp.maximum(i - 1, 0)]))
    def _():
        rows = w_ref.shape[0]
        step = min(CAST_ROWS, rows)
        for r0 in range(0, rows, step):
            cache[r0:r0 + step, :] = w_ref[r0:r0 + step, :].astype(BF16)


def _expert_up_kernel(be_ref, nu_ref, x_ref, wgu_ref, bgu_ref, o_ref, wcache):
    _refresh_cache(be_ref, wgu_ref, wcache)
    i = pl.program_id(0)

    @pl.when(i < nu_ref[0])
    def _():
        x_lo, x_hi = _unpack_rows(x_ref[...])
        x = jnp.concatenate([x_lo.astype(BF16), x_hi.astype(BF16)], axis=1)
        gu = jnp.dot(x, wcache[...], preferred_element_type=F32) + bgu_ref[...]
        glu, lin = _deinterleave(gu)
        glu = jnp.minimum(glu, SWIGLU_LIMIT)
        lin = jnp.clip(lin, -SWIGLU_LIMIT, SWIGLU_LIMIT)
        o_ref[...] = (glu * jax.nn.sigmoid(SWIGLU_ALPHA * glu) * (lin + 1.0)).astype(o_ref.dtype)

    @pl.when(i >= nu_ref[0])
    def _():
        o_ref[...] = jnp.zeros_like(o_ref)


def _expert_down_kernel(be_ref, nu_ref, a_ref, wd_ref, bd_ref, o_ref, wcache):
    _refresh_cache(be_ref, wd_ref, wcache)
    i = pl.program_id(0)

    @pl.when(i < nu_ref[0])
    def _():
        o_ref[...] = _pack_rows(jnp.dot(a_ref[...], wcache[...], preferred_element_type=F32) + bd_ref[...])

    @pl.when(i >= nu_ref[0])
    def _():
        o_ref[...] = jnp.zeros_like(o_ref)


def _experts(xs, block_e, n_used, wgu, bgu, wd, bd, l, bm):
    n_rows, Dw = xs.shape
    L, E, D, F2 = wgu.shape
    F = F2 // 2
    n_blocks = n_rows // bm

    def used(i, nu):
        return jnp.minimum(i, nu[0] - 1)

    up_spec = pltpu.PrefetchScalarGridSpec(
        num_scalar_prefetch=2,
        grid=(n_blocks,),
        in_specs=[
            pl.BlockSpec((bm, Dw), lambda i, be, nu: (used(i, nu), 0)),
            pl.BlockSpec((None, None, D, F2), lambda i, be, nu: (l, be[i], 0, 0)),
            pl.BlockSpec((None, None, 1, F2), lambda i, be, nu: (l, be[i], 0, 0)),
        ],
        out_specs=pl.BlockSpec((bm, F), lambda i, be, nu: (i, 0)),
        scratch_shapes=[pltpu.VMEM((D, F2), BF16)],
    )
    act = pl.pallas_call(
        _expert_up_kernel,
        grid_spec=up_spec,
        out_shape=jax.ShapeDtypeStruct((n_rows, F), BF16),
        compiler_params=_params(("arbitrary",)),
        name="expert_up",
    )(block_e, n_used, xs, wgu, bgu.reshape(L, E, 1, F2))
    down_spec = pltpu.PrefetchScalarGridSpec(
        num_scalar_prefetch=2,
        grid=(n_blocks,),
        in_specs=[
            pl.BlockSpec((bm, F), lambda i, be, nu: (i, 0)),
            pl.BlockSpec((None, None, F, D), lambda i, be, nu: (l, be[i], 0, 0)),
            pl.BlockSpec((None, None, 1, D), lambda i, be, nu: (l, be[i], 0, 0)),
        ],
        out_specs=pl.BlockSpec((bm, Dw), lambda i, be, nu: (i, 0)),
        scratch_shapes=[pltpu.VMEM((F, D), BF16)],
    )
    return pl.pallas_call(
        _expert_down_kernel,
        grid_spec=down_spec,
        out_shape=jax.ShapeDtypeStruct((n_rows, Dw), jnp.uint32),
        compiler_params=_params(("arbitrary",)),
        name="expert_down",
    )(block_e, n_used, act, wd, bd.reshape(L, E, 1, D))


def _combine_kernel(dest_ref, ys_ref, gatew_ref, x_ref, mod_ref, fg_ref, o_ref, buf, sem, *, m_tiles, final):
    tr = x_ref.shape[0]
    is_ctx = pl.program_id(0) < m_tiles

    def copy(r, k):
        return pltpu.make_async_copy(ys_ref.at[pl.ds(dest_ref[0, r * TOP_K + k], 1), :], buf.at[k, pl.ds(r, 1), :], sem)

    def issue(r, c):
        for k in range(TOP_K):
            copy(r, k).start()
        return c

    lax.fori_loop(0, tr, issue, 0)

    def drain(r, c):
        for k in range(TOP_K):
            copy(r, k).wait()
        return c

    lax.fori_loop(0, tr, drain, 0)

    half = x_ref.shape[1] // 2
    g = jnp.where(is_ctx, mod_ref[1:2, :], mod_ref[0:1, :])

    def piece(t, c):
        rs = pl.ds(pl.multiple_of(t * SUBLANES, SUBLANES), SUBLANES)
        gw = gatew_ref[rs, :]
        f_lo = f_hi = None
        for k in range(TOP_K):
            lo, hi = _unpack_rows(buf[k, rs, :])
            f_lo = gw[:, k:k + 1] * lo if f_lo is None else f_lo + gw[:, k:k + 1] * lo
            f_hi = gw[:, k:k + 1] * hi if f_hi is None else f_hi + gw[:, k:k + 1] * hi
        y_lo = x_ref[rs, 0:half] + g[:, 0:half] * f_lo
        y_hi = x_ref[rs, half:] + g[:, half:] * f_hi
        if final:
            ssq = jnp.sum(y_lo * y_lo, axis=-1, keepdims=True) + jnp.sum(y_hi * y_hi, axis=-1, keepdims=True)
            inv = lax.rsqrt(ssq / (2 * half) + NORM_EPS)
            y_lo = y_lo * inv * fg_ref[:, 0:half]
            y_hi = y_hi * inv * fg_ref[:, half:]
        o_ref[rs, 0:half] = y_lo
        o_ref[rs, half:] = y_hi
        return c

    lax.fori_loop(0, tr // SUBLANES, piece, 0)


def _combine(ys, dest, gatew, xs, mod, gate_idx, final_g, m, tr, final):
    T, D = xs.shape
    mt = m // tr
    out_rows = T - m if final else T
    out_map = (lambda i: (jnp.maximum(i - mt, 0), 0)) if final else (lambda i: (i, 0))
    return pl.pallas_call(
        functools.partial(_combine_kernel, m_tiles=mt, final=final),
        grid=(T // tr,),
        in_specs=[
            pl.BlockSpec((None, 1, tr * TOP_K), lambda i: (i, 0, 0), memory_space=pltpu.SMEM),
            pl.BlockSpec(memory_space=pl.ANY),
            pl.BlockSpec((tr, LANES), lambda i: (i, 0)),
            pl.BlockSpec((tr, D), lambda i: (i, 0)),
            pl.BlockSpec((SUBLANES, D), lambda i: (0, gate_idx)),
            pl.BlockSpec((1, D), lambda i: (0, 0)),
        ],
        out_specs=pl.BlockSpec((tr, D), out_map),
        out_shape=jax.ShapeDtypeStruct((out_rows, D), F32),
        scratch_shapes=[pltpu.VMEM((TOP_K, tr, D // 2), jnp.uint32), pltpu.SemaphoreType.DMA(())],
        compiler_params=_params(("arbitrary",)),
        name="combine",
    )(dest.reshape(T // tr, 1, tr * TOP_K), ys, gatew, xs, mod, final_g.reshape(1, D))


def kernel(x, c, ctx, c_ctx, w_ada, b_ada, norm1_g, norm2_g, w_in, lam_qk, subln_g, conv_w, conv_b, lru_gate_w,
           lru_gate_b, lru_lambda, na_rpb, w_branch, w_out, w_router, b_router, w_gu, b_gu, w_down, b_down, final_g):
    B, n, D = x.shape
    m = ctx.shape[1]
    assert B == 1 and n % (NA_QROWS * GRID_W) == 0
    L = w_ada.shape[0]
    E = w_router.shape[-1]
    T = m + n
    rows = n // GRID_W
    tr = 256
    assert m % tr == 0 and T % tr == 0 and m % (NA_QROWS * GRID_W) == 0
    tm = _tile(T, 1408, 128)
    tn = _tile(D, 512, LANES)
    bm = 256
    n_blocks = -(-(T * TOP_K + E * (bm - 1)) // bm)
    n_rows = n_blocks * bm

    xs = jnp.concatenate([ctx[0], x[0]], axis=0)
    mod_all = _ada(c, c_ctx, w_ada, b_ada)
    cos, sin = _rope_tables(m, n)

    for l in range(L):
        lambda_init = 0.8 - 0.6 * math.exp(-0.3 * l)
        mod = mod_all[l]
        h = _norm_mod(xs, norm1_g[l], mod, 0, 1, m, tr)
        z = _matmul(h, w_in, l, tm, tn, BF16)
        qk = _rope(z, cos, sin, tr)
        o_a = _attn_a(qk, z, lam_qk[l], subln_g[l], lambda_init, m, 512, 1536)
        o_b = _rglru(z, conv_w[l], conv_b[l], lru_gate_w[l], lru_gate_b[l], lru_lambda[l], m, 256)
        o_c = _na(z, na_rpb[l], m)
        y = _merge(o_a, o_b, o_c, w_branch, l, z, D, tm, tn)
        xs = _outproj(y, w_out, l, xs, mod, 2, m, tm, tn // 2)

        h2, idx, gatew, rank, cnt = _router(xs, norm2_g[l], mod, 3, 4, w_router[l], b_router[l], m, tr)
        counts = cnt[0, :E].astype(jnp.int32)
        padded = (counts + bm - 1) // bm * bm
        pend = jnp.cumsum(padded)
        pstart = pend - padded
        dest = (pstart[idx[:, :TOP_K]] + rank[:, :TOP_K]).astype(jnp.int32)
        block_start = jnp.arange(n_blocks, dtype=jnp.int32) * bm
        block_e = jnp.minimum(jnp.sum((pend[None, :] <= block_start[:, None]).astype(jnp.int32), axis=1), E - 1)
        n_used = (pend[-1:] // bm).astype(jnp.int32)

        xg = _dispatch(h2, dest, (pstart + counts).astype(jnp.int32), pend.astype(jnp.int32), n_used, n_rows, tr, bm)
        ys = _experts(xg, block_e, n_used, w_gu, b_gu, w_down, b_down, l, bm)
        xs = _combine(ys, dest, gatew, xs, mod, 5, final_g, m, tr, final=(l == L - 1))

    return xs[None]
```

```python
import functools
import math

import numpy as np
import jax
import jax.numpy as jnp
from jax import lax
from jax.experimental import pallas as pl
from jax.experimental.pallas import tpu as pltpu

F32 = jnp.float32
BF16 = jnp.bfloat16

GRID_W = 64
NORM_EPS = 1e-6
A_HEADS, A_DH = 8, 64
ROPE_PAIRS = A_DH // 4
ROPE_BASE = 10000.0
B_WIDTH, B_BW, B_CONV, CONV_LEFT, LRU_C = 1024, 128, 4, 2, 8.0
C_HEADS, C_DH, NA_ROWS, NA_COLS = 16, 64, 8, 16
N_BRANCH, BRANCH_W = 3, 1024
TOP_K = 4
SWIGLU_ALPHA, SWIGLU_LIMIT = 1.702, 7.0

LANES = 128
SUBLANES = 8
COL_AQ, COL_AK, COL_AV, COL_BX, COL_BY, COL_CQ, COL_CK, COL_CV, COL_G = 0, 8, 16, 24, 32, 40, 48, 56, 64
NEG_BIG = -1e30
VMEM_LIMIT = 56 * 1024 * 1024


def _params(sem):
    return pltpu.CompilerParams(dimension_semantics=sem, vmem_limit_bytes=VMEM_LIMIT)


def _tile(dim, target, mult):
    best = None
    for t in range(mult, min(dim, target) + 1, mult):
        if dim % t == 0:
            best = t
    assert best is not None, (dim, target, mult)
    return best


def _ada_kernel(c_ref, w_ref, b_ref, o_ref, acc_ref):
    k = pl.program_id(2)

    @pl.when(k == 0)
    def _():
        acc_ref[...] = jnp.zeros_like(acc_ref)

    cc = c_ref[...]
    s = cc * jax.nn.sigmoid(cc)
    hi = s.astype(BF16)
    lo = (s - hi.astype(F32)).astype(BF16)
    row = lax.broadcasted_iota(jnp.int32, s.shape, 0)
    lhs = jnp.where(row < 2, hi, lo)
    acc_ref[...] += jnp.dot(lhs, w_ref[...].astype(BF16), preferred_element_type=F32)

    @pl.when(k == pl.num_programs(2) - 1)
    def _():
        acc = acc_ref[...]
        o_ref[...] = acc + pltpu.roll(acc, SUBLANES - 2, 0) + b_ref[...]


def _ada(c, c_ctx, w_ada, b_ada):
    L, D, N = w_ada.shape
    cc = jnp.concatenate([c[:1], c_ctx[None], c[:1], c_ctx[None], jnp.zeros((4, D), F32)], axis=0)
    tk = _tile(D, 1024, LANES)
    tn = _tile(N, 2048, LANES)
    return pl.pallas_call(
        _ada_kernel,
        grid=(L, N // tn, D // tk),
        in_specs=[
            pl.BlockSpec((SUBLANES, tk), lambda l, j, k: (0, k)),
            pl.BlockSpec((None, tk, tn), lambda l, j, k: (l, k, j)),
            pl.BlockSpec((None, 1, tn), lambda l, j, k: (l, 0, j)),
        ],
        out_specs=pl.BlockSpec((None, SUBLANES, tn), lambda l, j, k: (l, 0, j)),
        out_shape=jax.ShapeDtypeStruct((L, SUBLANES, N), F32),
        scratch_shapes=[pltpu.VMEM((SUBLANES, tn), F32)],
        compiler_params=_params(("parallel", "parallel", "arbitrary")),
        name="ada",
    )(cc, w_ada, b_ada.reshape(L, 1, N))


def _norm_mod_kernel(x_ref, g_ref, sh_ref, sc_ref, o_ref, *, m_tiles):
    is_ctx = pl.program_id(0) < m_tiles
    x = x_ref[...]
    y = x * lax.rsqrt(jnp.mean(x * x, axis=-1, keepdims=True) + NORM_EPS) * g_ref[...]
    sh = jnp.where(is_ctx, sh_ref[1:2, :], sh_ref[0:1, :])
    sc = jnp.where(is_ctx, sc_ref[1:2, :], sc_ref[0:1, :])
    o_ref[...] = (y * (1.0 + sc) + sh).astype(o_ref.dtype)


def _norm_mod(xs, g, mod, shift_idx, scale_idx, m, tr):
    T, D = xs.shape
    return pl.pallas_call(
        functools.partial(_norm_mod_kernel, m_tiles=m // tr),
        grid=(T // tr,),
        in_specs=[
            pl.BlockSpec((tr, D), lambda i: (i, 0)),
            pl.BlockSpec((1, D), lambda i: (0, 0)),
            pl.BlockSpec((SUBLANES, D), lambda i: (0, shift_idx)),
            pl.BlockSpec((SUBLANES, D), lambda i: (0, scale_idx)),
        ],
        out_specs=pl.BlockSpec((tr, D), lambda i: (i, 0)),
        out_shape=jax.ShapeDtypeStruct((T, D), BF16),
        compiler_params=_params(("parallel",)),
        name="norm_mod",
    )(xs, g.reshape(1, D), mod, mod)


def _mm_kernel(a_ref, b_ref, o_ref):
    o_ref[...] = jnp.dot(a_ref[...], b_ref[...].astype(BF16), preferred_element_type=F32).astype(o_ref.dtype)


def _matmul(a, b_all, l, tm, tn, out_dtype):
    M, K = a.shape
    N = b_all.shape[-1]
    return pl.pallas_call(
        _mm_kernel,
        grid=(M // tm, N // tn),
        in_specs=[pl.BlockSpec((tm, K), lambda i, j: (i, 0)), pl.BlockSpec((None, K, tn), lambda i, j: (l, 0, j))],
        out_specs=pl.BlockSpec((tm, tn), lambda i, j: (i, j)),
        out_shape=jax.ShapeDtypeStruct((M, N), out_dtype),
        compiler_params=_params(("parallel", "arbitrary")),
        name="in_proj",
    )(a, b_all)


def _rope_tables(m, n):
    rows = n // GRID_W
    inv = jnp.asarray(ROPE_BASE, F32) ** (-jnp.arange(ROPE_PAIRS, dtype=F32) / ROPE_PAIRS)
    ang_r = jnp.arange(rows, dtype=F32)[:, None] * inv
    ang_c = jnp.arange(GRID_W, dtype=F32)[:, None] * inv

    def table(fn, sign_lo):
        tr_ = jnp.broadcast_to(fn(ang_r)[:, None, :], (rows, GRID_W, ROPE_PAIRS))
        tc_ = jnp.broadcast_to(fn(ang_c)[None, :, :], (rows, GRID_W, ROPE_PAIRS))
        vec = jnp.concatenate([sign_lo * tr_, tr_, sign_lo * tc_, tc_], axis=-1)
        full = jnp.concatenate([vec] * (LANES // A_DH), axis=-1).reshape(n, LANES)
        return full

    cos = jnp.concatenate([jnp.ones((m, LANES), F32), table(jnp.cos, 1.0)], axis=0)
    sin = jnp.concatenate([jnp.zeros((m, LANES), F32), table(jnp.sin, -1.0)], axis=0)
    return cos, sin


def _rope_kernel(z_ref, cos_ref, sin_ref, o_ref):
    scale = jnp.where(pl.program_id(1) == 0, A_DH ** -0.5, 1.0).astype(F32)
    cos = cos_ref[...]
    sin = sin_ref[...]
    lane = lax.broadcasted_iota(jnp.int32, cos.shape, 1)
    lo = (lane % (2 * ROPE_PAIRS)) < ROPE_PAIRS
    for hc in range(z_ref.shape[1] // LANES):
        x = z_ref[:, hc * LANES:(hc + 1) * LANES].astype(F32)
        partner = jnp.where(lo, pltpu.roll(x, LANES - ROPE_PAIRS, 1), pltpu.roll(x, ROPE_PAIRS, 1))
        o_ref[:, hc * LANES:(hc + 1) * LANES] = ((x * cos + partner * sin) * scale).astype(o_ref.dtype)


def _rope(z, cos, sin, tr):
    T = z.shape[0]
    W = A_HEADS * 2 * A_DH
    return pl.pallas_call(
        _rope_kernel,
        grid=(T // tr, 2),
        in_specs=[
            pl.BlockSpec((tr, W), lambda i, j: (i, j)),
            pl.BlockSpec((tr, LANES), lambda i, j: (i, 0)),
            pl.BlockSpec((tr, LANES), lambda i, j: (i, 0)),
        ],
        out_specs=pl.BlockSpec((tr, W), lambda i, j: (i, j)),
        out_shape=jax.ShapeDtypeStruct((T, 2 * W), BF16),
        compiler_params=_params(("parallel", "arbitrary")),
        name="rope",
    )(z, cos, sin)


def _attn_a_kernel(lam_ref, g_ref, q_ref, k_ref, v_ref, o_ref, *, m, T, tq, tk, lambda_init):
    lq = lam_ref[...]
    lam = (jnp.exp(jnp.sum(lq[0:1] * lq[1:2], axis=-1, keepdims=True))
           - jnp.exp(jnp.sum(lq[2:3] * lq[3:4], axis=-1, keepdims=True)) + lambda_init)

    def attend(row0, nq, bounds):
        qb = q_ref[pl.ds(row0, nq), :]
        lane = lax.broadcasted_iota(jnp.int32, qb.shape, 1)
        zero = jnp.zeros_like(qb)
        qs = jnp.concatenate([jnp.where(lane < A_DH, qb, zero), jnp.where(lane >= A_DH, qb, zero)], axis=0)

        def scores(lo, hi):
            return lax.dot_general(qs, k_ref[lo:hi, :], (((1,), (1,)), ((), ())), preferred_element_type=F32)

        carry = (jnp.full((2 * nq, 1), NEG_BIG, F32), jnp.zeros((2 * nq, 1), F32), jnp.zeros((2 * nq, LANES), F32))
        s = scores(*bounds[0])
        for c, (lo, hi) in enumerate(bounds):
            s_next = scores(*bounds[c + 1]) if c + 1 < len(bounds) else None
            m_i, l_i, acc = carry
            m_new = jnp.maximum(m_i, jnp.max(s, axis=-1, keepdims=True))
            alpha = jnp.exp(m_i - m_new)
            p = jnp.exp(s - m_new)
            l_new = alpha * l_i + jnp.sum(p, axis=-1, keepdims=True)
            acc = alpha * acc + jnp.dot(p.astype(BF16), v_ref[lo:hi, :], preferred_element_type=F32)
            carry = (m_new, l_new, acc)
            s = s_next
        _, l_i, acc = carry
        o = acc / l_i
        d = o[:nq] - lam * o[nq:]
        y = d * lax.rsqrt(jnp.mean(d * d, axis=-1, keepdims=True) + NORM_EPS) * g_ref[...]
        o_ref[pl.ds(row0, nq), :] = (y * (1.0 - lambda_init)).astype(o_ref.dtype)

    @pl.when(pl.program_id(1) == 0)
    def _():
        attend(0, m, [(0, m)])

    attend(pl.multiple_of(m + pl.program_id(1) * tq, math.gcd(m, tq)), tq,
           [(lo, min(lo + tk, T)) for lo in range(0, T, tk)])


def _attn_a(qk, z, lam_qk, subln_g, lambda_init, m, tq, tk):
    T = z.shape[0]
    assert (T - m) % tq == 0
    col = lambda base: pl.BlockSpec((T, LANES), lambda h, i: (0, base + h))
    return pl.pallas_call(
        functools.partial(_attn_a_kernel, m=m, T=T, tq=tq, tk=tk, lambda_init=lambda_init),
        grid=(A_HEADS, (T - m) // tq),
        in_specs=[
            pl.BlockSpec((4, A_DH), lambda h, i: (0, 0)),
            pl.BlockSpec((1, 2 * A_DH), lambda h, i: (0, 0)),
            col(0), col(A_HEADS), col(COL_AV),
        ],
        out_specs=pl.BlockSpec((T, LANES), lambda h, i: (0, h)),
        out_shape=jax.ShapeDtypeStruct((T, A_HEADS * 2 * A_DH), BF16),
        compiler_params=_params(("parallel", "arbitrary")),
        name="diff_attn",
    )(lam_qk, subln_g.reshape(1, 2 * A_DH), qk, qk, z)


def _rglru_kernel(bx_ref, by_ref, cw_ref, cb_ref, gw_ref, gb_ref, lam_ref, o_ref, xpad, hf, *, m, T, tc):
    nchunk = T // tc
    mc = m // tc
    nb = tc // SUBLANES
    halo = SUBLANES
    xpad[0:halo, :] = jnp.zeros((halo, LANES), F32)
    xpad[halo + T:2 * halo + T, :] = jnp.zeros((halo, LANES), F32)
    xpad[halo:halo + T, :] = bx_ref[...].astype(F32)

    row = lax.broadcasted_iota(jnp.int32, (tc, LANES), 0)
    sub = lax.broadcasted_iota(jnp.int32, (nb, SUBLANES, LANES), 1)
    cw = cw_ref[...]
    cb = cb_ref[...]

    def coeffs(j, d):
        r0 = j * tc
        left_cut = r0 == m
        right_cut = (r0 + tc) == m
        x_m2 = xpad[pl.ds(r0 + halo - 2, tc), :]
        x_m1 = xpad[pl.ds(r0 + halo - 1, tc), :]
        x_0 = xpad[pl.ds(r0 + halo, tc), :]
        x_p1 = xpad[pl.ds(r0 + halo + 1, tc), :]
        x_m2 = jnp.where(jnp.logical_and(left_cut, row < 2), 0.0, x_m2)
        x_m1 = jnp.where(jnp.logical_and(left_cut, row < 1), 0.0, x_m1)
        x_p1 = jnp.where(jnp.logical_and(right_cut, row >= tc - 1), 0.0, x_p1)
        u = cb + x_m2 * cw[0:1] + x_m1 * cw[1:2] + x_0 * cw[2:3] + x_p1 * cw[3:4]
        ub = u.astype(BF16)
        gr = jnp.dot(ub, gw_ref[d, 0].astype(BF16), preferred_element_type=F32) + gb_ref[d, 0]
        gi = jnp.dot(ub, gw_ref[d, 1].astype(BF16), preferred_element_type=F32) + gb_ref[d, 1]
        r = jax.nn.sigmoid(gr)
        i = jax.nn.sigmoid(gi)
        log_a = (-LRU_C * jax.nn.softplus(-lam_ref[d])) * r
        a = jnp.exp(log_a)
        b = jnp.sqrt(-jnp.tanh(log_a) * (a * a + 1.0)) * (i * u)
        return a, b

    def chunk_scan(a, b, h, rev):
        A = a.reshape(nb, SUBLANES, LANES)
        B = b.reshape(nb, SUBLANES, LANES)
        for s in (1, 2, 4):
            shift = SUBLANES - s if rev else s
            msk = (sub < SUBLANES - s) if rev else (sub >= s)
            Ap = pltpu.roll(A, shift, 1)
            Bp = pltpu.roll(B, shift, 1)
            B = jnp.where(msk, A * Bp + B, B)
            A = jnp.where(msk, A * Ap, A)
        outs = [None] * nb
        order = range(nb - 1, -1, -1) if rev else range(nb)
        for jb in order:
            hb = A[jb] * h + B[jb]
            outs[jb] = hb
            edge = hb[0:1, :] if rev else hb[SUBLANES - 1:SUBLANES, :]
            h = jnp.broadcast_to(edge, (SUBLANES, LANES))
        return jnp.concatenate(outs, axis=0), h

    def fwd_body(j, h):
        a, b = coeffs(j, 0)
        hc, h = chunk_scan(a, b, h, False)
        hf[pl.ds(pl.multiple_of(j * tc, tc), tc), :] = hc
        return h

    lax.fori_loop(0, nchunk, fwd_body, jnp.zeros((SUBLANES, LANES), F32))

    def bwd_body(s, h):
        j = jnp.where(s < mc, mc - 1 - s, nchunk - 1 - (s - mc))
        a, b = coeffs(j, 1)
        hc, h = chunk_scan(a, b, h, True)
        r0 = pl.multiple_of(j * tc, tc)
        by = by_ref[pl.ds(r0, tc), :].astype(F32)
        o_ref[pl.ds(r0, tc), :] = ((hf[pl.ds(r0, tc), :] + hc) * jax.nn.gelu(by)).astype(o_ref.dtype)
        return h

    lax.fori_loop(0, nchunk, bwd_body, jnp.zeros((SUBLANES, LANES), F32))


def _rglru(z, conv_w, conv_b, gate_w, gate_b, lru_lambda, m, tc):
    T = z.shape[0]
    nblk = B_WIDTH // B_BW
    return pl.pallas_call(
        functools.partial(_rglru_kernel, m=m, T=T, tc=tc),
        grid=(nblk,),
        in_specs=[
            pl.BlockSpec((T, LANES), lambda c: (0, COL_BX + c)),
            pl.BlockSpec((T, LANES), lambda c: (0, COL_BY + c)),
            pl.BlockSpec((B_CONV, LANES), lambda c: (0, c)),
            pl.BlockSpec((1, LANES), lambda c: (0, c)),
            pl.BlockSpec((2, 2, None, B_BW, B_BW), lambda c: (0, 0, c, 0, 0)),
            pl.BlockSpec((2, 2, None, 1, B_BW), lambda c: (0, 0, c, 0, 0)),
            pl.BlockSpec((2, 1, LANES), lambda c: (0, 0, c)),
        ],
        out_specs=pl.BlockSpec((T, LANES), lambda c: (0, c)),
        out_shape=jax.ShapeDtypeStruct((T, B_WIDTH), BF16),
        scratch_shapes=[pltpu.VMEM((T + 2 * SUBLANES, LANES), F32), pltpu.VMEM((T, LANES), F32)],
        compiler_params=_params(("parallel",)),
        name="rglru",
    )(z, z, conv_w, conv_b.reshape(1, B_WIDTH), gate_w, gate_b.reshape(2, 2, nblk, 1, B_BW),
      lru_lambda.reshape(2, 1, B_WIDTH))


NA_QROWS = 4
NA_KROWS = NA_QROWS + NA_ROWS - 1
NA_UNROLL = 4


def _na_base(g, rows):
    return jnp.clip(NA_QROWS * g - NA_ROWS // 2, 0, rows - NA_KROWS)


def _na_patterns(rows):
    sigs, reps, pat_of_g = {}, [], []
    for g in range(rows // NA_QROWS):
        base = int(np.clip(NA_QROWS * g - NA_ROWS // 2, 0, rows - NA_KROWS))
        sig = (base - NA_QROWS * g,) + tuple(
            int(np.clip(r - NA_ROWS // 2, 0, rows - NA_ROWS)) - r for r in range(NA_QROWS * g, NA_QROWS * (g + 1)))
        if sig not in sigs:
            sigs[sig] = len(reps)
            reps.append(g)
        pat_of_g.append(sigs[sig])
    return reps, pat_of_g


def _na_row_rel(rows, reps):
    rel = []
    for g in reps:
        base = int(np.clip(NA_QROWS * g - NA_ROWS // 2, 0, rows - NA_KROWS))
        per_rho = []
        for rho in range(NA_QROWS):
            r = NA_QROWS * g + rho
            r0 = int(np.clip(r - NA_ROWS // 2, 0, rows - NA_ROWS))
            per_rho.append([base + j - r + NA_ROWS - 1 if r0 <= base + j < r0 + NA_ROWS else None
                            for j in range(NA_KROWS)])
        rel.append(per_rho)
    return rel


def _na_col_selectors():
    n_cr = 2 * NA_COLS - 1
    sel = np.zeros((2, n_cr, GRID_W, LANES), np.float32)
    mask = np.zeros((2, GRID_W, LANES), np.float32)
    neg = np.zeros((2, GRID_W, LANES), np.float32)
    for half in range(2):
        neg[half, :, half * GRID_W:(half + 1) * GRID_W] = NEG_BIG
        mask[half, :, half * GRID_W:(half + 1) * GRID_W] = NEG_BIG
        for c in range(GRID_W):
            c0 = int(np.clip(c - NA_COLS // 2, 0, GRID_W - NA_COLS))
            for kc in range(c0, c0 + NA_COLS):
                sel[half, kc - c + NA_COLS - 1, c, half * GRID_W + kc] = 1.0
                mask[half, c, half * GRID_W + kc] = 0.0
    return sel, mask, neg


def _na_bias_kernel(rpb_ref, sel_ref, mask_ref, neg_ref, o_ref, blocks, *, rel):
    h = pl.program_id(0)
    n_rr, n_cr = 2 * NA_ROWS - 1, 2 * NA_COLS - 1

    def build(r, carry):
        for half in range(2):
            acc = mask_ref[half]
            for col in range(n_cr):
                acc = acc + rpb_ref[h, r, col] * sel_ref[half, col]
            blocks[half, r] = acc
        return carry

    lax.fori_loop(0, n_rr, build, 0)
    for p, per_rho in enumerate(rel):
        for rho, per_j in enumerate(per_rho):
            rs = slice(rho * GRID_W, (rho + 1) * GRID_W)
            for jj in range(0, NA_KROWS, 2):
                lo = blocks[0, per_j[jj]] if per_j[jj] is not None else neg_ref[0]
                if jj + 1 < NA_KROWS:
                    hi = blocks[1, per_j[jj + 1]] if per_j[jj + 1] is not None else neg_ref[1]
                    o_ref[p, rs, jj * GRID_W:(jj + 2) * GRID_W] = lo + hi
                else:
                    o_ref[p, rs, jj * GRID_W:(jj + 1) * GRID_W] = lo[:, 0:GRID_W]


def _na_bias(rpb, rows, reps):
    H = rpb.shape[0]
    n_rr = 2 * NA_ROWS - 1
    sel, mask, neg = _na_col_selectors()
    tq, nk = NA_QROWS * GRID_W, NA_KROWS * GRID_W
    whole = lambda a: pl.BlockSpec(a.shape, lambda h: (0,) * a.ndim)
    return pl.pallas_call(
        functools.partial(_na_bias_kernel, rel=_na_row_rel(rows, reps)),
        grid=(H,),
        in_specs=[pl.BlockSpec(memory_space=pltpu.SMEM), whole(sel), whole(mask), whole(neg)],
        out_specs=pl.BlockSpec((len(reps), None, tq, nk), lambda h: (0, h, 0, 0)),
        out_shape=jax.ShapeDtypeStruct((len(reps), H, tq, nk), F32),
        scratch_shapes=[pltpu.VMEM((2, n_rr, GRID_W, LANES), F32)],
        compiler_params=_params(("parallel",)),
        name="nbr_bias",
    )(rpb.astype(F32), jnp.asarray(sel), jnp.asarray(mask), jnp.asarray(neg))


def _na_kernel(pat_ref, q_ref, k_ref, v_ref, bias_ref, o_ref, *, m, rows):
    tq = NA_QROWS * GRID_W
    nk = NA_KROWS * GRID_W
    lane = lax.broadcasted_iota(jnp.int32, (tq, LANES), 1)
    nt = (((1,), (1,)), ((), ()))

    def stacked_q(row0):
        qb = q_ref[pl.ds(row0, tq), :] * jnp.asarray(C_DH ** -0.5, BF16)
        zero = jnp.zeros_like(qb)
        return jnp.concatenate([jnp.where(lane < C_DH, qb, zero), jnp.where(lane >= C_DH, qb, zero)], axis=0)

    def emit(row0, o):
        o_ref[pl.ds(row0, tq), :] = jnp.where(lane < C_DH, o[:tq], o[tq:]).astype(o_ref.dtype)

    def softmax_pv(parts):
        mx = None
        for s, _ in parts:
            pm = jnp.max(s, axis=-1, keepdims=True)
            mx = pm if mx is None else jnp.maximum(mx, pm)
        l = None
        o = None
        for s, v in parts:
            p = jnp.exp(s - mx)
            ps = jnp.sum(p, axis=-1, keepdims=True)
            po = jnp.dot(p.astype(BF16), v, preferred_element_type=F32)
            l = ps if l is None else l + ps
            o = po if o is None else o + po
        return o / l

    for c in range(m // tq):
        emit(c * tq, softmax_pv([(lax.dot_general(stacked_q(c * tq), k_ref[0:m, :], nt, preferred_element_type=F32),
                                  v_ref[0:m, :])]))

    def group(g, carry):
        row0 = pl.multiple_of(m + g * tq, tq)
        off = pl.multiple_of(m + _na_base(g, rows) * GRID_W, GRID_W)
        qs = stacked_q(row0)
        bias = bias_ref[pat_ref[g]].reshape(2 * tq, nk)
        s_nb = lax.dot_general(qs, k_ref[pl.ds(off, nk), :], nt, preferred_element_type=F32) + bias
        s_cx = lax.dot_general(qs, k_ref[0:m, :], nt, preferred_element_type=F32)
        emit(row0, softmax_pv([(s_nb, v_ref[pl.ds(off, nk), :]), (s_cx, v_ref[0:m, :])]))
        return carry

    lax.fori_loop(0, rows // NA_QROWS, group, 0, unroll=NA_UNROLL)


def _na(z, rpb, m):
    T = z.shape[0]
    rows = (T - m) // GRID_W
    tq = NA_QROWS * GRID_W
    assert m % tq == 0 and (rows // NA_QROWS) % NA_UNROLL == 0
    reps, pat_of_g = _na_patterns(rows)
    bias = _na_bias(rpb, rows, reps)
    pat = jnp.asarray(pat_of_g, jnp.int32)
    col = lambda base: pl.BlockSpec((T, LANES), lambda h, pat: (0, base + h))
    grid_spec = pltpu.PrefetchScalarGridSpec(
        num_scalar_prefetch=1,
        grid=(C_HEADS // 2,),
        in_specs=[col(COL_CQ), col(COL_CK), col(COL_CV),
                  pl.BlockSpec((len(reps), 2, tq, NA_KROWS * GRID_W), lambda h, pat: (0, h, 0, 0))],
        out_specs=pl.BlockSpec((T, LANES), lambda h, pat: (0, h)),
    )
    return pl.pallas_call(
        functools.partial(_na_kernel, m=m, rows=rows),
        grid_spec=grid_spec,
        out_shape=jax.ShapeDtypeStruct((T, C_HEADS * C_DH), BF16),
        compiler_params=_params(("parallel",)),
        name="nbr_attn",
    )(pat, z, z, z, bias)


def _merge_kernel(oa_ref, ob_ref, oc_ref, wb_ref, g0_ref, g1_ref, g2_ref, o_ref):
    acc = None
    for i, (o, g) in enumerate(((oa_ref, g0_ref), (ob_ref, g1_ref), (oc_ref, g2_ref))):
        t = jax.nn.sigmoid(g[...].astype(F32)) * jnp.dot(o[...], wb_ref[i].astype(BF16), preferred_element_type=F32)
        acc = t if acc is None else acc + t
    o_ref[...] = acc.astype(o_ref.dtype)


def _merge(oa, ob, oc, wb_all, l, z, D, tm, tn):
    T = oa.shape[0]
    gcol = COL_G * LANES // tn
    nd = D // tn
    o_spec = pl.BlockSpec((tm, BRANCH_W), lambda i, j: (i, 0))
    return pl.pallas_call(
        _merge_kernel,
        grid=(T // tm, D // tn),
        in_specs=[o_spec, o_spec, o_spec,
                  pl.BlockSpec((None, N_BRANCH, BRANCH_W, tn), lambda i, j: (l, 0, 0, j)),
                  pl.BlockSpec((tm, tn), lambda i, j: (i, gcol + j)),
                  pl.BlockSpec((tm, tn), lambda i, j: (i, gcol + nd + j)),
                  pl.BlockSpec((tm, tn), lambda i, j: (i, gcol + 2 * nd + j))],
        out_specs=pl.BlockSpec((tm, tn), lambda i, j: (i, j)),
        out_shape=jax.ShapeDtypeStruct((T, D), BF16),
        compiler_params=_params(("parallel", "arbitrary")),
        name="merge",
    )(oa, ob, oc, wb_all, z, z, z)


def _outproj_kernel(y_ref, w_ref, x_ref, gate_ref, o_ref, *, m):
    tm = y_ref.shape[0]
    r = jnp.dot(y_ref[...], w_ref[...].astype(BF16), preferred_element_type=F32)
    row = pl.program_id(0) * tm + lax.broadcasted_iota(jnp.int32, r.shape, 0)
    g = jnp.where(row < m, gate_ref[1:2, :], gate_ref[0:1, :])
    o_ref[...] = x_ref[...] + g * r


def _outproj(y, w_all, l, xs, mod, gate_idx, m, tm, tn):
    T, D = xs.shape
    nd = D // tn
    return pl.pallas_call(
        functools.partial(_outproj_kernel, m=m),
        grid=(T // tm, nd),
        in_specs=[pl.BlockSpec((tm, D), lambda i, j: (i, 0), pipeline_mode=pl.Buffered(1)),
                  pl.BlockSpec((None, D, tn), lambda i, j: (l, 0, j)),
                  pl.BlockSpec((tm, tn), lambda i, j: (i, j)),
                  pl.BlockSpec((SUBLANES, tn), lambda i, j: (0, gate_idx * nd + j))],
        out_specs=pl.BlockSpec((tm, tn), lambda i, j: (i, j)),
        out_shape=jax.ShapeDtypeStruct((T, D), F32),
        compiler_params=_params(("parallel", "arbitrary")),
        name="out_proj",
    )(y, w_all, xs, mod)


HI_MASK = 0xFFFF0000


def _pack_rows(v):
    half = v.shape[1] // 2
    u = lax.bitcast_convert_type(v.astype(BF16).astype(F32), jnp.uint32)
    return (u[:, half:] & jnp.uint32(HI_MASK)) | (u[:, :half] >> 16)


def _unpack_rows(w):
    lo = lax.bitcast_convert_type(w << 16, F32)
    hi = lax.bitcast_convert_type(w & jnp.uint32(HI_MASK), F32)
    return lo, hi

def _router_kernel(x_ref, g_ref, sh_ref, sc_ref, wr_ref, br_ref, h_ref, idx_ref, gate_ref, rank_ref, cnt_ref,
                   run_ref, *, m_tiles, n_experts):
    i = pl.program_id(0)
    is_ctx = i < m_tiles

    @pl.when(i == 0)
    def _():
        run_ref[...] = jnp.zeros_like(run_ref)

    x = x_ref[...]
    tr = x.shape[0]
    y = x * lax.rsqrt(jnp.mean(x * x, axis=-1, keepdims=True) + NORM_EPS) * g_ref[...]
    sh = jnp.where(is_ctx, sh_ref[1:2, :], sh_ref[0:1, :])
    sc = jnp.where(is_ctx, sc_ref[1:2, :], sc_ref[0:1, :])
    h = y * (1.0 + sc) + sh
    hb = h.astype(BF16)
    h_ref[...] = _pack_rows(h)

    h_lo = (h - hb.astype(F32)).astype(BF16)
    w = wr_ref[...]
    w_hi = w.astype(BF16)
    w_lo = (w - w_hi.astype(F32)).astype(BF16)
    logits = (jnp.dot(hb, w_hi, preferred_element_type=F32) + jnp.dot(h_lo, w_hi, preferred_element_type=F32)
              + jnp.dot(hb, w_lo, preferred_element_type=F32)) + br_ref[...]
    lane = lax.broadcasted_iota(jnp.int32, logits.shape, 1)
    work = jnp.where(lane < n_experts, logits, NEG_BIG)

    vals, idxs = [], []
    for _ in range(TOP_K):
        mx = jnp.max(work, axis=-1, keepdims=True)
        ix = jnp.min(jnp.where(work == mx, lane, LANES), axis=-1, keepdims=True)
        vals.append(mx)
        idxs.append(ix)
        work = jnp.where(lane == ix, NEG_BIG, work)
    es = [jnp.exp(v - vals[0]) for v in vals]
    den = es[0] + es[1] + es[2] + es[3]

    onehot = jnp.zeros(logits.shape, F32)
    for ix in idxs:
        onehot = onehot + jnp.where(lane == ix, 1.0, 0.0)
    r_i = lax.broadcasted_iota(jnp.int32, (tr, tr), 0)
    c_i = lax.broadcasted_iota(jnp.int32, (tr, tr), 1)
    tri = jnp.where(c_i < r_i, 1.0, 0.0).astype(BF16)
    before = jnp.dot(tri, onehot.astype(BF16), preferred_element_type=F32) + run_ref[...]

    idx_o = jnp.zeros(logits.shape, jnp.int32)
    gate_o = jnp.zeros(logits.shape, F32)
    rank_o = jnp.zeros(logits.shape, jnp.int32)
    for k in range(TOP_K):
        rk = jnp.sum(jnp.where(lane == idxs[k], before, 0.0), axis=-1, keepdims=True)
        idx_o = jnp.where(lane == k, idxs[k], idx_o)
        gate_o = jnp.where(lane == k, es[k] / den, gate_o)
        rank_o = jnp.where(lane == k, rk.astype(jnp.int32), rank_o)
    idx_ref[...] = idx_o
    gate_ref[...] = gate_o
    rank_ref[...] = rank_o
    run_ref[...] += jnp.sum(onehot, axis=0, keepdims=True)
    cnt_ref[...] = jnp.broadcast_to(run_ref[...], cnt_ref.shape)


def _router(xs, g, mod, shift_idx, scale_idx, w_router, b_router, m, tr):
    T, D = xs.shape
    E = w_router.shape[1]
    wr = jnp.pad(w_router, ((0, 0), (0, LANES - E)))
    br = jnp.pad(b_router, (0, LANES - E)).reshape(1, LANES)
    slab = pl.BlockSpec((tr, LANES), lambda i: (i, 0))
    return pl.pallas_call(
        functools.partial(_router_kernel, m_tiles=m // tr, n_experts=E),
        grid=(T // tr,),
        in_specs=[
            pl.BlockSpec((tr, D), lambda i: (i, 0)),
            pl.BlockSpec((1, D), lambda i: (0, 0)),
            pl.BlockSpec((SUBLANES, D), lambda i: (0, shift_idx)),
            pl.BlockSpec((SUBLANES, D), lambda i: (0, scale_idx)),
            pl.BlockSpec((D, LANES), lambda i: (0, 0)),
            pl.BlockSpec((1, LANES), lambda i: (0, 0)),
        ],
        out_specs=[pl.BlockSpec((tr, D // 2), lambda i: (i, 0)), slab, slab, slab,
                   pl.BlockSpec((SUBLANES, LANES), lambda i: (0, 0))],
        out_shape=[jax.ShapeDtypeStruct((T, D // 2), jnp.uint32), jax.ShapeDtypeStruct((T, LANES), jnp.int32),
                   jax.ShapeDtypeStruct((T, LANES), F32), jax.ShapeDtypeStruct((T, LANES), jnp.int32),
                   jax.ShapeDtypeStruct((SUBLANES, LANES), F32)],
        scratch_shapes=[pltpu.VMEM((1, LANES), F32)],
        compiler_params=_params(("arbitrary",)),
        name="router",
    )(xs, g.reshape(1, D), mod, mod, wr, br)


def _dispatch_kernel(fill_lo_ref, fill_hi_ref, nu_ref, dest_ref, h_ref, xs_ref, zblk, sem, zsem, *, bm):
    tr = h_ref.shape[0]

    @pl.when(pl.program_id(0) == 0)
    def _():
        zblk[...] = jnp.zeros_like(zblk)
        n_blocks = xs_ref.shape[0] // bm

        def zero_row(r):
            return pltpu.make_async_copy(zblk.at[pl.ds(0, 1), :], xs_ref.at[pl.ds(r, 1), :], zsem)

        def zero_block(b):
            return pltpu.make_async_copy(zblk, xs_ref.at[pl.ds(pl.multiple_of(b * bm, bm), bm), :], zsem)

        def per_expert(e, c):
            lax.fori_loop(fill_lo_ref[e], fill_hi_ref[e], lambda r, c2: (zero_row(r).start(), c2)[1], 0)
            return c

        lax.fori_loop(0, fill_lo_ref.shape[0], per_expert, 0)
        lax.fori_loop(nu_ref[0], n_blocks, lambda b, c: (zero_block(b).start(), c)[1], 0)

        def per_expert_wait(e, c):
            lax.fori_loop(fill_lo_ref[e], fill_hi_ref[e], lambda r, c2: (zero_row(r).wait(), c2)[1], 0)
            return c

        lax.fori_loop(0, fill_lo_ref.shape[0], per_expert_wait, 0)
        lax.fori_loop(nu_ref[0], n_blocks, lambda b, c: (zero_block(b).wait(), c)[1], 0)

    def copy(r, k):
        return pltpu.make_async_copy(h_ref.at[pl.ds(r, 1), :], xs_ref.at[pl.ds(dest_ref[0, r * TOP_K + k], 1), :], sem)

    def issue(r, c):
        for k in range(TOP_K):
            copy(r, k).start(priority=k % 2)
        return c

    lax.fori_loop(0, tr, issue, 0)

    def drain(r, c):
        for k in range(TOP_K):
            copy(r, k).wait()
        return c

    lax.fori_loop(0, tr, drain, 0)


def _dispatch(h, dest, fill_lo, fill_hi, n_used, n_rows, tr, bm):
    T, D = h.shape
    grid_spec = pltpu.PrefetchScalarGridSpec(
        num_scalar_prefetch=3,
        grid=(T // tr,),
        in_specs=[
            pl.BlockSpec((None, 1, tr * TOP_K), lambda i, lo, hi, nu: (i, 0, 0), memory_space=pltpu.SMEM),
            pl.BlockSpec((tr, D), lambda i, lo, hi, nu: (i, 0)),
        ],
        out_specs=pl.BlockSpec(memory_space=pl.ANY),
        scratch_shapes=[pltpu.VMEM((bm, D), h.dtype), pltpu.SemaphoreType.DMA(()), pltpu.SemaphoreType.DMA(())],
    )
    return pl.pallas_call(
        functools.partial(_dispatch_kernel, bm=bm),
        grid_spec=grid_spec,
        out_shape=jax.ShapeDtypeStruct((n_rows, D), h.dtype),
        compiler_params=_params(("arbitrary",)),
        name="dispatch",
    )(fill_lo, fill_hi, n_used, dest.reshape(T // tr, 1, tr * TOP_K), h)


def _deinterleave(gu):
    r, width = gu.shape
    lane = lax.broadcasted_iota(jnp.int32, (r, LANES), 1)
    idx_even = (2 * lane) % LANES
    idx_odd = idx_even + 1
    first = lane < LANES // 2
    even, odd = [], []
    for c in range(width // (2 * LANES)):
        xa = gu[:, 2 * LANES * c:2 * LANES * c + LANES]
        xb = gu[:, 2 * LANES * c + LANES:2 * LANES * (c + 1)]
        even.append(jnp.where(first, jnp.take_along_axis(xa, idx_even, axis=1), jnp.take_along_axis(xb, idx_even, axis=1)))
        odd.append(jnp.where(first, jnp.take_along_axis(xa, idx_odd, axis=1), jnp.take_along_axis(xb, idx_odd, axis=1)))
    return jnp.concatenate(even, axis=1), jnp.concatenate(odd, axis=1)


CAST_ROWS = 512


def _refresh_cache(be_ref, w_ref, cache):
    i = pl.program_id(0)

    @pl.when(jnp.logical_or(i == 0, be_ref[i] != be_ref[jnp.maximum(i - 1, 0)]))
    def _():
        rows = w_ref.shape[0]
        step = min(CAST_ROWS, rows)
        for r0 in range(0, rows, step):
            cache[r0:r0 + step, :] = w_ref[r0:r0 + step, :].astype(BF16)


def _expert_up_kernel(be_ref, nu_ref, x_ref, wgu_ref, bgu_ref, o_ref, wcache):
    _refresh_cache(be_ref, wgu_ref, wcache)
    i = pl.program_id(0)

    @pl.when(i < nu_ref[0])
    def _():
        x_lo, x_hi = _unpack_rows(x_ref[...])
        x = jnp.concatenate([x_lo.astype(BF16), x_hi.astype(BF16)], axis=1)
        gu = jnp.dot(x, wcache[...], preferred_element_type=F32) + bgu_ref[...]
        glu, lin = _deinterleave(gu)
        glu = jnp.minimum(glu, SWIGLU_LIMIT)
        lin = jnp.clip(lin, -SWIGLU_LIMIT, SWIGLU_LIMIT)
        o_ref[...] = (glu * jax.nn.sigmoid(SWIGLU_ALPHA * glu) * (lin + 1.0)).astype(o_ref.dtype)

    @pl.when(i >= nu_ref[0])
    def _():
        o_ref[...] = jnp.zeros_like(o_ref)


def _expert_down_kernel(be_ref, nu_ref, a_ref, wd_ref, bd_ref, o_ref, wcache):
    _refresh_cache(be_ref, wd_ref, wcache)
    i = pl.program_id(0)

    @pl.when(i < nu_ref[0])
    def _():
        o_ref[...] = _pack_rows(jnp.dot(a_ref[...], wcache[...], preferred_element_type=F32) + bd_ref[...])

    @pl.when(i >= nu_ref[0])
    def _():
        o_ref[...] = jnp.zeros_like(o_ref)


def _experts(xs, block_e, n_used, wgu, bgu, wd, bd, l, bm):
    n_rows, Dw = xs.shape
    L, E, D, F2 = wgu.shape
    F = F2 // 2
    n_blocks = n_rows // bm

    def used(i, nu):
        return jnp.minimum(i, nu[0] - 1)

    up_spec = pltpu.PrefetchScalarGridSpec(
        num_scalar_prefetch=2,
        grid=(n_blocks,),
        in_specs=[
            pl.BlockSpec((bm, Dw), lambda i, be, nu: (used(i, nu), 0)),
            pl.BlockSpec((None, None, D, F2), lambda i, be, nu: (l, be[i], 0, 0)),
            pl.BlockSpec((None, None, 1, F2), lambda i, be, nu: (l, be[i], 0, 0)),
        ],
        out_specs=pl.BlockSpec((bm, F), lambda i, be, nu: (i, 0)),
        scratch_shapes=[pltpu.VMEM((D, F2), BF16)],
    )
    act = pl.pallas_call(
        _expert_up_kernel,
        grid_spec=up_spec,
        out_shape=jax.ShapeDtypeStruct((n_rows, F), BF16),
        compiler_params=_params(("arbitrary",)),
        name="expert_up",
    )(block_e, n_used, xs, wgu, bgu.reshape(L, E, 1, F2))
    down_spec = pltpu.PrefetchScalarGridSpec(
        num_scalar_prefetch=2,
        grid=(n_blocks,),
        in_specs=[
            pl.BlockSpec((bm, F), lambda i, be, nu: (i, 0)),
            pl.BlockSpec((None, None, F, D), lambda i, be, nu: (l, be[i], 0, 0)),
            pl.BlockSpec((None, None, 1, D), lambda i, be, nu: (l, be[i], 0, 0)),
        ],
        out_specs=pl.BlockSpec((bm, Dw), lambda i, be, nu: (i, 0)),
        scratch_shapes=[pltpu.VMEM((F, D), BF16)],
    )
    return pl.pallas_call(
        _expert_down_kernel,
        grid_spec=down_spec,
        out_shape=jax.ShapeDtypeStruct((n_rows, Dw), jnp.uint32),
        compiler_params=_params(("arbitrary",)),
        name="expert_down",
    )(block_e, n_used, act, wd, bd.reshape(L, E, 1, D))


def _combine_kernel(dest_ref, ys_ref, gatew_ref, x_ref, mod_ref, fg_ref, o_ref, buf, sem, *, m_tiles, final):
    tr = x_ref.shape[0]
    is_ctx = pl.program_id(0) < m_tiles

    def copy(r, k):
        return pltpu.make_async_copy(ys_ref.at[pl.ds(dest_ref[0, r * TOP_K + k], 1), :], buf.at[k, pl.ds(r, 1), :], sem)

    def issue(r, c):
        for k in range(TOP_K):
            copy(r, k).start(priority=k % 2)
        return c

    lax.fori_loop(0, tr, issue, 0)

    def drain(r, c):
        for k in range(TOP_K):
            copy(r, k).wait()
        return c

    lax.fori_loop(0, tr, drain, 0)

    half = x_ref.shape[1] // 2
    g = jnp.where(is_ctx, mod_ref[1:2, :], mod_ref[0:1, :])

    def piece(t, c):
        rs = pl.ds(pl.multiple_of(t * SUBLANES, SUBLANES), SUBLANES)
        gw = gatew_ref[rs, :]
        f_lo = f_hi = None
        for k in range(TOP_K):
            lo, hi = _unpack_rows(buf[k, rs, :])
            f_lo = gw[:, k:k + 1] * lo if f_lo is None else f_lo + gw[:, k:k + 1] * lo
            f_hi = gw[:, k:k + 1] * hi if f_hi is None else f_hi + gw[:, k:k + 1] * hi
        y_lo = x_ref[rs, 0:half] + g[:, 0:half] * f_lo
        y_hi = x_ref[rs, half:] + g[:, half:] * f_hi
        if final:
            ssq = jnp.sum(y_lo * y_lo, axis=-1, keepdims=True) + jnp.sum(y_hi * y_hi, axis=-1, keepdims=True)
            inv = lax.rsqrt(ssq / (2 * half) + NORM_EPS)
            y_lo = y_lo * inv * fg_ref[:, 0:half]
            y_hi = y_hi * inv * fg_ref[:, half:]
        o_ref[rs, 0:half] = y_lo
        o_ref[rs, half:] = y_hi
        return c

    lax.fori_loop(0, tr // SUBLANES, piece, 0, unroll=2)


def _combine(ys, dest, gatew, xs, mod, gate_idx, final_g, m, tr, final):
    T, D = xs.shape
    mt = m // tr
    out_rows = T - m if final else T
    out_map = (lambda i: (jnp.maximum(i - mt, 0), 0)) if final else (lambda i: (i, 0))
    return pl.pallas_call(
        functools.partial(_combine_kernel, m_tiles=mt, final=final),
        grid=(T // tr,),
        in_specs=[
            pl.BlockSpec((None, 1, tr * TOP_K), lambda i: (i, 0, 0), memory_space=pltpu.SMEM),
            pl.BlockSpec(memory_space=pl.ANY),
            pl.BlockSpec((tr, LANES), lambda i: (i, 0)),
            pl.BlockSpec((tr, D), lambda i: (i, 0)),
            pl.BlockSpec((SUBLANES, D), lambda i: (0, gate_idx)),
            pl.BlockSpec((1, D), lambda i: (0, 0)),
        ],
        out_specs=pl.BlockSpec((tr, D), out_map),
        out_shape=jax.ShapeDtypeStruct((out_rows, D), F32),
        scratch_shapes=[pltpu.VMEM((TOP_K, tr, D // 2), jnp.uint32), pltpu.SemaphoreType.DMA(())],
        compiler_params=_params(("arbitrary",)),
        name="combine",
    )(dest.reshape(T // tr, 1, tr * TOP_K), ys, gatew, xs, mod, final_g.reshape(1, D))


def kernel(x, c, ctx, c_ctx, w_ada, b_ada, norm1_g, norm2_g, w_in, lam_qk, subln_g, conv_w, conv_b, lru_gate_w,
           lru_gate_b, lru_lambda, na_rpb, w_branch, w_out, w_router, b_router, w_gu, b_gu, w_down, b_down, final_g):
    B, n, D = x.shape
    m = ctx.shape[1]
    assert B == 1 and n % (NA_QROWS * GRID_W) == 0
    L = w_ada.shape[0]
    E = w_router.shape[-1]
    T = m + n
    rows = n // GRID_W
    tr = 256
    assert m % tr == 0 and T % tr == 0 and m % (NA_QROWS * GRID_W) == 0
    tm = _tile(T, 1408, 128)
    tn = _tile(D, 512, LANES)
    bm = 256
    n_blocks = -(-(T * TOP_K + E * (bm - 1)) // bm)
    n_rows = n_blocks * bm

    xs = jnp.concatenate([ctx[0], x[0]], axis=0)
    mod_all = _ada(c, c_ctx, w_ada, b_ada)
    cos, sin = _rope_tables(m, n)

    for l in range(L):
        lambda_init = 0.8 - 0.6 * math.exp(-0.3 * l)
        mod = mod_all[l]
        h = _norm_mod(xs, norm1_g[l], mod, 0, 1, m, tr)
        z = _matmul(h, w_in, l, tm, tn, BF16)
        qk = _rope(z, cos, sin, tr)
        o_a = _attn_a(qk, z, lam_qk[l], subln_g[l], lambda_init, m, 512, 1536)
        o_b = _rglru(z, conv_w[l], conv_b[l], lru_gate_w[l], lru_gate_b[l], lru_lambda[l], m, 256)
        o_c = _na(z, na_rpb[l], m)
        y = _merge(o_a, o_b, o_c, w_branch, l, z, D, tm, tn)
        xs = _outproj(y, w_out, l, xs, mod, 2, m, tm, tn)

        h2, idx, gatew, rank, cnt = _router(xs, norm2_g[l], mod, 3, 4, w_router[l], b_router[l], m, tr)
        counts = cnt[0, :E].astype(jnp.int32)
        padded = (counts + bm - 1) // bm * bm
        pend = jnp.cumsum(padded)
        pstart = pend - padded
        dest = (pstart[idx[:, :TOP_K]] + rank[:, :TOP_K]).astype(jnp.int32)
        block_start = jnp.arange(n_blocks, dtype=jnp.int32) * bm
        block_e = jnp.minimum(jnp.sum((pend[None, :] <= block_start[:, None]).astype(jnp.int32), axis=1), E - 1)
        n_used = (pend[-1:] // bm).astype(jnp.int32)

        xg = _dispatch(h2, dest, (pstart + counts).astype(jnp.int32), pend.astype(jnp.int32), n_used, n_rows, tr, bm)
        ys = _experts(xg, block_e, n_used, w_gu, b_gu, w_down, b_down, l, bm)
        xs = _combine(ys, dest, gatew, xs, mod, 5, final_g, m, tr, final=(l == L - 1))

    return xs[None]
```
